```python
import math
import jax, jax.numpy as jnp
from jax import lax
import numpy as np

D_MODEL = 2048
BATCH = 8
SEQ = 2048
DEPTH = 1
DEC_BATCH = 128
DEC_SEQ = 8
PAST_LEN = 16384
PAGE_SIZE = 128

N_META = 16
ROPE_THETA = 500000.0
NORM_EPS = 1e-6
SUBLN_EPS = 1e-5
Q_BLOCK = 128
NEG_INF = -1e30

MLA_HEADS = 8
MLA_QK_NOPE = 128
MLA_ROPE = 64
MLA_V = 128
MLA_Q_RANK = 512
MLA_KV_RANK = 256
MLA_WIDTH = MLA_HEADS * MLA_V
MLA_SCALE = (MLA_QK_NOPE + MLA_ROPE) ** -0.5

DIFF_HEADS = 8
DIFF_KV_HEADS = 2
DIFF_REP = DIFF_HEADS // DIFF_KV_HEADS
DIFF_HEAD_DIM = 64
DIFF_ROT = DIFF_HEAD_DIM // 4
DIFF_WIDTH = DIFF_HEADS * 2 * DIFF_HEAD_DIM
DIFF_Q_WIDTH = DIFF_HEADS * 2 * DIFF_HEAD_DIM
DIFF_KV_WIDTH = DIFF_KV_HEADS * 2 * DIFF_HEAD_DIM
DIFF_SCALE = DIFF_HEAD_DIM ** -0.5

MIX_WIDTH = MLA_WIDTH + DIFF_WIDTH
IN_SIZES = (MLA_Q_RANK, MLA_KV_RANK, MLA_ROPE, DIFF_Q_WIDTH, DIFF_KV_WIDTH, DIFF_KV_WIDTH)
IN_WIDTH = MLA_Q_RANK + MLA_KV_RANK + MLA_ROPE + DIFF_Q_WIDTH + 2 * DIFF_KV_WIDTH
FFN_HIDDEN = -(-8 * D_MODEL // (3 * 256)) * 256

kernel_name = "hymba_mla_diffattn_decoder_step"


def rmsnorm(x, g, eps=NORM_EPS):
    xf = x.astype(jnp.float32)
    y = xf * lax.rsqrt(jnp.mean(xf * xf, axis=-1, keepdims=True) + eps)
    return (y * g.astype(jnp.float32)).astype(x.dtype)


def rope(x, pos, rot_dim):
    half = rot_dim // 2
    inv = ROPE_THETA ** (-jnp.arange(half, dtype=jnp.float32) * 2.0 / rot_dim)
    ang = pos.astype(jnp.float32)[:, None] * inv
    shape = (1, ang.shape[0]) + (1,) * (x.ndim - 3) + (half,)
    cos = jnp.cos(ang).reshape(shape)
    sin = jnp.sin(ang).reshape(shape)
    xf = x.astype(jnp.float32)
    x1, x2, rest = xf[..., :half], xf[..., half:rot_dim], xf[..., rot_dim:]
    out = jnp.concatenate([x1 * cos - x2 * sin, x2 * cos + x1 * sin, rest], axis=-1)
    return out.astype(x.dtype)


def _expand_mask(mask, ndim):
    return mask.reshape((1, mask.shape[0]) + (1,) * (ndim - 3) + (mask.shape[1],))


def mla_scores(q, k):
    q_abs, q_pe = q
    c, kr = k
    s = (jnp.einsum('bthc,bsc->bths', q_abs, c, preferred_element_type=jnp.float32)
         + jnp.einsum('bthr,bsr->bths', q_pe, kr, preferred_element_type=jnp.float32))
    return s * MLA_SCALE


def mla_mix(p, k):
    c = k[0]
    return jnp.einsum('bths,bsc->bthc', p.astype(c.dtype), c)


def diff_scores(q, k):
    kk = k[0]
    return jnp.einsum('btgrmd,bsgmd->btgrms', q, kk, preferred_element_type=jnp.float32) * DIFF_SCALE


def diff_mix(p, k):
    v = k[1]
    return jnp.einsum('btgrms,bsgv->btgrmv', p.astype(v.dtype), v)


def prompt_attention(q, k, score_fn, mix_fn):
    leaf = jax.tree_util.tree_leaves(q)[0]
    tp = leaf.shape[1]
    nb = tp // Q_BLOCK
    qb = jax.tree_util.tree_map(
        lambda a: jnp.moveaxis(a.reshape((a.shape[0], nb, Q_BLOCK) + a.shape[2:]), 1, 0), q)
    kpos = jnp.arange(tp)

    def one_block(args):
        i, qi = args
        qpos = i * Q_BLOCK + jnp.arange(Q_BLOCK)
        s = score_fn(qi, k)
        s = jnp.where(_expand_mask(qpos[:, None] >= kpos[None, :], s.ndim), s, NEG_INF)
        return mix_fn(jax.nn.softmax(s, axis=-1), k)

    out = lax.map(one_block, (jnp.arange(nb), qb))
    out = jnp.moveaxis(out, 0, 1)
    return out.reshape((out.shape[0], tp) + out.shape[3:])


def sample_attention(q, k_new, pools, page_table, score_fn, mix_fn):
    tq = jax.tree_util.tree_leaves(q)[0].shape[1]
    s = score_fn(q, k_new)
    s = jnp.where(_expand_mask(jnp.tril(jnp.ones((tq, tq), bool)), s.ndim), s, NEG_INF)
    m = jnp.max(s, axis=-1)
    p = jnp.exp(s - m[..., None])
    l = jnp.sum(p, axis=-1)
    acc = mix_fn(p, k_new).astype(jnp.float32)

    def step(carry, pages):
        m, l, acc = carry
        kb = jax.tree_util.tree_map(lambda a: a[pages], pools)
        s = score_fn(q, kb)
        m_new = jnp.maximum(m, jnp.max(s, axis=-1))
        alpha = jnp.exp(m - m_new)
        p = jnp.exp(s - m_new[..., None])
        l = alpha * l + jnp.sum(p, axis=-1)
        acc = alpha[..., None] * acc + mix_fn(p, kb).astype(jnp.float32)
        return (m_new, l, acc), None

    (m, l, acc), _ = lax.scan(step, (m, l, acc), page_table.T)
    return acc / l[..., None]


def mixer_inputs(xn, pos, w_in, g_q_a, w_q_b, g_kv_a, w_uk):
    B, T, _ = xn.shape
    z = xn @ w_in
    parts = []
    off = 0
    for size in IN_SIZES:
        parts.append(z[..., off:off + size])
        off += size
    q_a, kv_a, kr_raw, dq, dk, dv = parts
    q = jnp.einsum('btr,rhe->bthe', rmsnorm(q_a, g_q_a), w_q_b)
    q_pe = rope(q[..., MLA_QK_NOPE:], pos, MLA_ROPE)
    q_abs = jnp.einsum('bthn,chn->bthc', q[..., :MLA_QK_NOPE], w_uk)
    c = rmsnorm(kv_a, g_kv_a)
    kr = rope(kr_raw, pos, MLA_ROPE)
    dq = rope(dq.reshape(B, T, DIFF_KV_HEADS, DIFF_REP, 2, DIFF_HEAD_DIM), pos, DIFF_ROT)
    dk = rope(dk.reshape(B, T, DIFF_KV_HEADS, 2, DIFF_HEAD_DIM), pos, DIFF_ROT)
    dv = dv.reshape(B, T, DIFF_KV_HEADS, 2 * DIFF_HEAD_DIM)
    return (q_abs, q_pe), (c, kr), dq, (dk, dv)


def mixer_output(mla_lat, diff_o, w_uv, g_mla_out, lam, lam_init, g_subln, w_out, dtype):
    B, T = mla_lat.shape[:2]
    mla = jnp.einsum('bthc,chv->bthv', mla_lat.astype(dtype), w_uv).reshape(B, T, MLA_WIDTH)
    mla = rmsnorm(mla, g_mla_out)
    d = diff_o.astype(jnp.float32)
    d = d[..., 0, :] - lam * d[..., 1, :]
    d = rmsnorm(d, g_subln, SUBLN_EPS) * (1.0 - lam_init)
    d = d.reshape(B, T, DIFF_WIDTH).astype(dtype)
    return jnp.concatenate([mla, d], axis=-1) @ w_out


def swiglu(xn, w_gate, w_up, w_down):
    return (jax.nn.silu(xn @ w_gate) * (xn @ w_up)) @ w_down


def setup_inputs(seed: int = 0) -> dict:
    key = jax.random.key(seed)
    ks = jax.random.split(key, 32)
    f32 = jnp.float32
    n_pages = PAST_LEN // PAGE_SIZE
    n_used = DEC_BATCH * n_pages
    n_phys = n_used + max(1, n_used // 4)
    nrm = lambda k, shape, scale=1.0: jax.random.normal(k, shape, f32) * scale
    gain = lambda k, shape: 1.0 + 0.01 * jax.random.normal(k, shape, f32)
    page_table = jax.random.permutation(ks[6], n_phys)[:n_used].reshape(DEC_BATCH, n_pages).astype(jnp.int32)
    return {
        "x_prompt": nrm(ks[0], (BATCH, SEQ, D_MODEL)),
        "x_sample": nrm(ks[1], (DEC_BATCH, DEC_SEQ, D_MODEL)),
        "cache_mla_latent": nrm(ks[2], (DEPTH, n_phys, PAGE_SIZE, MLA_KV_RANK)),
        "cache_mla_krope": nrm(ks[3], (DEPTH, n_phys, PAGE_SIZE, MLA_ROPE)),
        "cache_diff_k": nrm(ks[4], (DEPTH, n_phys, PAGE_SIZE, DIFF_KV_HEADS, 2, DIFF_HEAD_DIM)),
        "cache_diff_v": nrm(ks[5], (DEPTH, n_phys, PAGE_SIZE, DIFF_KV_HEADS, 2 * DIFF_HEAD_DIM)),
        "page_table": page_table,
        "meta_tokens": nrm(ks[7], (N_META, D_MODEL)),
        "g_attn": gain(ks[8], (DEPTH, D_MODEL)),
        "w_in": nrm(ks[9], (DEPTH, D_MODEL, IN_WIDTH), D_MODEL ** -0.5),
        "g_q_a": gain(ks[10], (DEPTH, MLA_Q_RANK)),
        "w_q_b": nrm(ks[11], (DEPTH, MLA_Q_RANK, MLA_HEADS, MLA_QK_NOPE + MLA_ROPE), MLA_Q_RANK ** -0.5),
        "g_kv_a": gain(ks[12], (DEPTH, MLA_KV_RANK)),
        "w_uk": nrm(ks[13], (DEPTH, MLA_KV_RANK, MLA_HEADS, MLA_QK_NOPE), MLA_KV_RANK ** -0.5),
        "w_uv": nrm(ks[14], (DEPTH, MLA_KV_RANK, MLA_HEADS, MLA_V), MLA_KV_RANK ** -0.5),
        "g_mla_out": gain(ks[15], (DEPTH, MLA_WIDTH)),
        "lambda_q1": nrm(ks[16], (DEPTH, DIFF_HEAD_DIM), 0.1),
        "lambda_k1": nrm(ks[17], (DEPTH, DIFF_HEAD_DIM), 0.1),
        "lambda_q2": nrm(ks[18], (DEPTH, DIFF_HEAD_DIM), 0.1),
        "lambda_k2": nrm(ks[19], (DEPTH, DIFF_HEAD_DIM), 0.1),
        "g_subln": gain(ks[20], (DEPTH, 2 * DIFF_HEAD_DIM)),
        "w_out": nrm(ks[21], (DEPTH, MIX_WIDTH, D_MODEL), MIX_WIDTH ** -0.5),
        "g_ffn": gain(ks[22], (DEPTH, D_MODEL)),
        "w_gate": nrm(ks[23], (DEPTH, D_MODEL, FFN_HIDDEN), D_MODEL ** -0.5),
        "w_up": nrm(ks[24], (DEPTH, D_MODEL, FFN_HIDDEN), D_MODEL ** -0.5),
        "w_down": nrm(ks[25], (DEPTH, FFN_HIDDEN, D_MODEL), FFN_HIDDEN ** -0.5),
        "g_final": gain(ks[26], (D_MODEL,)),
    }


def reference(x_prompt, x_sample, cache_mla_latent, cache_mla_krope, cache_diff_k, cache_diff_v,
              page_table, meta_tokens, g_attn, w_in, g_q_a, w_q_b, g_kv_a, w_uk, w_uv, g_mla_out,
              lambda_q1, lambda_k1, lambda_q2, lambda_k2, g_subln, w_out, g_ffn, w_gate, w_up,
              w_down, g_final):
    f32 = jnp.float32
    B = x_prompt.shape[0]
    T = N_META + x_prompt.shape[1]
    TP = -(-T // Q_BLOCK) * Q_BLOCK
    dtype = x_prompt.dtype
    hp = jnp.concatenate(
        [jnp.broadcast_to(meta_tokens.astype(dtype)[None], (B, N_META, D_MODEL)), x_prompt], axis=1)
    hs = x_sample
    pos_p = jnp.arange(TP)
    pos_s = PAST_LEN + jnp.arange(x_sample.shape[1])

    lat_p, kr_p, dk_p, dv_p = [], [], [], []
    lat_s, kr_s, dk_s, dv_s = [], [], [], []
    for l in range(DEPTH):
        lam_init = 0.8 - 0.6 * math.exp(-0.3 * l)
        lam = (jnp.exp(jnp.sum(lambda_q1[l].astype(f32) * lambda_k1[l].astype(f32)))
               - jnp.exp(jnp.sum(lambda_q2[l].astype(f32) * lambda_k2[l].astype(f32))) + lam_init)
        proj = (w_in[l], g_q_a[l], w_q_b[l], g_kv_a[l], w_uk[l])
        outp = (w_uv[l], g_mla_out[l], lam, lam_init, g_subln[l], w_out[l])

        xn = jnp.pad(rmsnorm(hp, g_attn[l]), ((0, 0), (0, TP - T), (0, 0)))
        mq, mk, dq, dkv = mixer_inputs(xn, pos_p, *proj)
        mla_lat = prompt_attention(mq, mk, mla_scores, mla_mix)[:, :T]
        diff_o = prompt_attention(dq, dkv, diff_scores, diff_mix)[:, :T]
        hp = hp + mixer_output(mla_lat, diff_o, *outp, dtype)
        hp = hp + swiglu(rmsnorm(hp, g_ffn[l]), w_gate[l], w_up[l], w_down[l])
        lat_p.append(mk[0][:, :T]); kr_p.append(mk[1][:, :T])
        dk_p.append(dkv[0][:, :T]); dv_p.append(dkv[1][:, :T])

        xn = rmsnorm(hs, g_attn[l])
        mq, mk, dq, dkv = mixer_inputs(xn, pos_s, *proj)
        mla_lat = sample_attention(mq, mk, (cache_mla_latent[l], cache_mla_krope[l]), page_table,
                                   mla_scores, mla_mix)
        diff_o = sample_attention(dq, dkv, (cache_diff_k[l], cache_diff_v[l]), page_table,
                                  diff_scores, diff_mix)
        hs = hs + mixer_output(mla_lat, diff_o, *outp, dtype)
        hs = hs + swiglu(rmsnorm(hs, g_ffn[l]), w_gate[l], w_up[l], w_down[l])
        lat_s.append(mk[0]); kr_s.append(mk[1]); dk_s.append(dkv[0]); dv_s.append(dkv[1])

    y_prompt = rmsnorm(hp, g_final)[:, N_META:]
    y_sample = rmsnorm(hs, g_final)
    return (y_prompt, y_sample,
            jnp.stack(lat_p), jnp.stack(kr_p), jnp.stack(dk_p), jnp.stack(dv_p),
            jnp.stack(lat_s), jnp.stack(kr_s), jnp.stack(dk_s), jnp.stack(dv_s))
```

```python
import functools
import math

import jax
import jax.numpy as jnp
from jax import lax
from jax.experimental import pallas as pl
from jax.experimental.pallas import tpu as pltpu

F32 = jnp.float32
BF16 = jnp.bfloat16

N_META = 16
ROPE_THETA = 500000.0
NORM_EPS = 1e-6
SUBLN_EPS = 1e-5
NEG_INF = -1e30
PAGE_SIZE = 128

MLA_HEADS = 8
MLA_QK_NOPE = 128
MLA_ROPE = 64
MLA_V = 128
MLA_Q_RANK = 512
MLA_KV_RANK = 256
MLA_WIDTH = MLA_HEADS * MLA_V
MLA_SCALE = (MLA_QK_NOPE + MLA_ROPE) ** -0.5

DIFF_HEADS = 8
DIFF_KV_HEADS = 2
DIFF_REP = DIFF_HEADS // DIFF_KV_HEADS
DIFF_HEAD_DIM = 64
DIFF_ROT = DIFF_HEAD_DIM // 4
DIFF_WIDTH = DIFF_HEADS * 2 * DIFF_HEAD_DIM
DIFF_KV_WIDTH = DIFF_KV_HEADS * 2 * DIFF_HEAD_DIM
DIFF_SCALE = DIFF_HEAD_DIM ** -0.5
DIFF_ROWS = DIFF_KV_HEADS * DIFF_REP * 2

LANES = 128
VMEM_LIMIT = 56 * 1024 * 1024

OFF_QA = 0
OFF_KVA = OFF_QA + MLA_Q_RANK
OFF_DQ = OFF_KVA + MLA_KV_RANK
OFF_DK = OFF_DQ + DIFF_WIDTH
OFF_DV = OFF_DK + DIFF_KV_WIDTH
OFF_KR = OFF_DV + DIFF_KV_WIDTH
IN_WIDTH_PADDED = OFF_KR + LANES


def _rms(x, g, eps):
    return x * lax.rsqrt(jnp.mean(x * x, axis=-1, keepdims=True) + eps) * g


def _dot(a, b):
    return jnp.dot(a, b, preferred_element_type=F32)


def _dot_nt(a, b):
    return lax.dot_general(a, b, (((1,), (1,)), ((), ())), preferred_element_type=F32)


def _rope_chunk(x, cos, sin_lo, sin_hi, half):
    return (x * cos + pltpu.roll(x, LANES - half, 1) * sin_lo
            + pltpu.roll(x, half, 1) * sin_hi)


def _causal(s, n_tok, tok0, col0):
    assert n_tok & (n_tok - 1) == 0
    tok = tok0 + (lax.broadcasted_iota(jnp.int32, s.shape, 0) & (n_tok - 1))
    col = col0 + lax.broadcasted_iota(jnp.int32, s.shape, 1)
    return jnp.where(col <= tok, s, NEG_INF)


def _const_spec(shape):
    return pl.BlockSpec(shape, lambda *_: (0,) * len(shape))


def _proj_kernel(x_ref, g_attn_ref, w_in_ref, g_q_ref, g_kv_ref, w_qb_ref, w_uk_ref,
                 cos_m_ref, slo_m_ref, shi_m_ref, cos_d_ref, slo_d_ref, shi_d_ref,
                 qabs_ref, qpe_ref, dq_ref, c_ref, kr_ref, dk_ref, dv_ref,
                 cb_ref, krb_ref, dkb_ref, dvb_ref):
    xn = _rms(x_ref[...], g_attn_ref[...], NORM_EPS).astype(BF16)
    z = _dot(xn, w_in_ref[...])

    cos_m, slo_m, shi_m = cos_m_ref[...], slo_m_ref[...], shi_m_ref[...]
    cos_d, slo_d, shi_d = cos_d_ref[...], slo_d_ref[...], shi_d_ref[...]

    c = _rms(z[:, OFF_KVA:OFF_KVA + MLA_KV_RANK], g_kv_ref[...], NORM_EPS)
    c_ref[...] = c
    cb_ref[...] = c.astype(BF16)
    kr = _rope_chunk(z[:, OFF_KR:OFF_KR + LANES], cos_m, slo_m, shi_m, MLA_ROPE // 2)
    kr_ref[...] = kr[:, :MLA_ROPE]
    krb_ref[...] = kr[:, :MLA_ROPE].astype(BF16)

    qn = _rms(z[:, OFF_QA:OFF_QA + MLA_Q_RANK], g_q_ref[...], NORM_EPS).astype(BF16)
    q = _dot(qn, w_qb_ref[...])
    nope_w = MLA_HEADS * MLA_QK_NOPE
    for h in range(MLA_HEADS):
        qh = q[:, h * MLA_QK_NOPE:(h + 1) * MLA_QK_NOPE].astype(BF16)
        qabs_ref[:, h * MLA_KV_RANK:(h + 1) * MLA_KV_RANK] = _dot(qh, w_uk_ref[h]).astype(qabs_ref.dtype)
    for j in range(MLA_HEADS * MLA_ROPE // LANES):
        qc = q[:, nope_w + j * LANES:nope_w + (j + 1) * LANES]
        qpe_ref[:, j * LANES:(j + 1) * LANES] = _rope_chunk(
            qc, cos_m, slo_m, shi_m, MLA_ROPE // 2).astype(qpe_ref.dtype)

    for j in range(DIFF_WIDTH // LANES):
        xc = z[:, OFF_DQ + j * LANES:OFF_DQ + (j + 1) * LANES]
        dq_ref[:, j * LANES:(j + 1) * LANES] = _rope_chunk(
            xc, cos_d, slo_d, shi_d, DIFF_ROT // 2).astype(dq_ref.dtype)
    for j in range(DIFF_KV_WIDTH // LANES):
        xc = z[:, OFF_DK + j * LANES:OFF_DK + (j + 1) * LANES]
        dk = _rope_chunk(xc, cos_d, slo_d, shi_d, DIFF_ROT // 2)
        dk_ref[:, j * LANES:(j + 1) * LANES] = dk
        dkb_ref[:, j * LANES:(j + 1) * LANES] = dk.astype(BF16)
    dv = z[:, OFF_DV:OFF_DV + DIFF_KV_WIDTH]
    dv_ref[...] = dv
    dvb_ref[...] = dv.astype(BF16)


def _proj(x, tables, tm, q_dtype, w):
    rows, d = x.shape
    t_tab = tables[0].shape[0]
    n_tab = t_tab // tm
    row = lambda i: (i, 0)
    tab = lambda i: (i % n_tab, 0)
    out_widths = [(MLA_HEADS * MLA_KV_RANK, q_dtype), (MLA_HEADS * MLA_ROPE, q_dtype),
                  (DIFF_WIDTH, q_dtype), (MLA_KV_RANK, F32), (MLA_ROPE, F32),
                  (DIFF_KV_WIDTH, F32), (DIFF_KV_WIDTH, F32), (MLA_KV_RANK, BF16),
                  (MLA_ROPE, BF16), (DIFF_KV_WIDTH, BF16), (DIFF_KV_WIDTH, BF16)]
    return pl.pallas_call(
        _proj_kernel,
        grid=(rows // tm,),
        in_specs=[pl.BlockSpec((tm, d), row),
                  _const_spec((1, d)),
                  _const_spec(w["w_in"].shape),
                  _const_spec((1, MLA_Q_RANK)),
                  _const_spec((1, MLA_KV_RANK)),
                  _const_spec(w["w_qb"].shape),
                  _const_spec(w["w_uk"].shape)]
                 + [pl.BlockSpec((tm, LANES), tab)] * 6,
        out_specs=[pl.BlockSpec((tm, n), row) for n, _ in out_widths],
        out_shape=[jax.ShapeDtypeStruct((rows, n), dt) for n, dt in out_widths],
        compiler_params=pltpu.CompilerParams(
            dimension_semantics=("arbitrary",), vmem_limit_bytes=VMEM_LIMIT),
        name="proj",
    )(x, w["g_attn"], w["w_in"], w["g_q_a"], w["g_kv_a"], w["w_qb"], w["w_uk"], *tables)


def _mla_attn_kernel(qabs_ref, qpe_ref, cb_ref, krb_ref, cmeta_ref, krmeta_ref, o_ref,
                     qs_ref, qps_ref, m_ref, l_ref, acc_ref, *, bq):
    qi = pl.program_id(1)
    for h in range(MLA_HEADS):
        qs_ref[h * bq:(h + 1) * bq, :] = qabs_ref[:, h * MLA_KV_RANK:(h + 1) * MLA_KV_RANK]
        qps_ref[h * bq:(h + 1) * bq, :] = qpe_ref[:, h * MLA_ROPE:(h + 1) * MLA_ROPE]
    qs = qs_ref[...]
    qps = qps_ref[...]

    cm = cmeta_ref[...]
    s = (_dot_nt(qs, cm) + _dot_nt(qps, krmeta_ref[...])) * MLA_SCALE
    m0 = jnp.max(s, axis=-1, keepdims=True)
    p = jnp.exp(s - m0)
    m_ref[...] = m0
    l_ref[...] = jnp.sum(p, axis=-1, keepdims=True)
    acc_ref[...] = _dot(p.astype(BF16), cm)

    def attend(kb, masked):
        start = pl.multiple_of(kb * bq, bq)
        k = cb_ref[pl.ds(start, bq), :]
        s = (_dot_nt(qs, k) + _dot_nt(qps, krb_ref[pl.ds(start, bq), :])) * MLA_SCALE
        if masked:
            s = _causal(s, bq, 0, 0)
        m_prev = m_ref[...]
        m_new = jnp.maximum(m_prev, jnp.max(s, axis=-1, keepdims=True))
        alpha = jnp.exp(m_prev - m_new)
        p = jnp.exp(s - m_new)
        l_ref[...] = alpha * l_ref[...] + jnp.sum(p, axis=-1, keepdims=True)
        acc_ref[...] = alpha * acc_ref[...] + _dot(p.astype(BF16), k)
        m_ref[...] = m_new

    def body(kb, carry):
        attend(kb, False)
        return carry

    lax.fori_loop(0, qi, body, 0)
    attend(qi, True)

    for h in range(MLA_HEADS):
        rows = slice(h * bq, (h + 1) * bq)
        o_ref[:, h * MLA_KV_RANK:(h + 1) * MLA_KV_RANK] = (
            acc_ref[rows, :] / l_ref[rows, :]).astype(o_ref.dtype)


def _mla_attn(qabs, qpe, cb, krb, cmeta, krmeta, batch, seq, bq):
    nq = seq // bq
    rows = MLA_HEADS * bq
    qrow = lambda b, i: (b * nq + i, 0)
    kv = lambda b, i: (b, 0)
    return pl.pallas_call(
        functools.partial(_mla_attn_kernel, bq=bq),
        grid=(batch, nq),
        in_specs=[pl.BlockSpec((bq, MLA_HEADS * MLA_KV_RANK), qrow),
                  pl.BlockSpec((bq, MLA_HEADS * MLA_ROPE), qrow),
                  pl.BlockSpec((seq, MLA_KV_RANK), kv),
                  pl.BlockSpec((seq, MLA_ROPE), kv),
                  _const_spec(cmeta.shape),
                  _const_spec(krmeta.shape)],
        out_specs=pl.BlockSpec((bq, MLA_HEADS * MLA_KV_RANK), qrow),
        out_shape=jax.ShapeDtypeStruct((batch * seq, MLA_HEADS * MLA_KV_RANK), BF16),
        scratch_shapes=[pltpu.VMEM((rows, MLA_KV_RANK), BF16),
                        pltpu.VMEM((rows, MLA_ROPE), BF16),
                        pltpu.VMEM((rows, 1), F32),
                        pltpu.VMEM((rows, 1), F32),
                        pltpu.VMEM((rows, MLA_KV_RANK), F32)],
        compiler_params=pltpu.CompilerParams(
            dimension_semantics=("arbitrary", "arbitrary"), vmem_limit_bytes=VMEM_LIMIT),
        name="mla_attn",
    )(qabs, qpe, cb, krb, cmeta, krmeta)


def _lambda(lq1_ref, lk1_ref, lq2_ref, lk2_ref, lam_init):
    a = jnp.sum(lq1_ref[...] * lk1_ref[...], axis=-1, keepdims=True)
    b = jnp.sum(lq2_ref[...] * lk2_ref[...], axis=-1, keepdims=True)
    return jnp.exp(a) - jnp.exp(b) + lam_init


def _fill_diff_queries(qd_ref, dq_ref, n_tok):
    qd_ref[...] = jnp.zeros(qd_ref.shape, qd_ref.dtype)
    lane = lax.broadcasted_iota(jnp.int32, (n_tok, LANES), 1)
    for g in range(DIFF_KV_HEADS):
        for r in range(DIFF_REP):
            col = (g * DIFF_REP + r) * LANES
            x = dq_ref[:, col:col + LANES] * DIFF_SCALE
            for m in range(2):
                row = ((g * DIFF_REP + r) * 2 + m) * n_tok
                keep = (lane < DIFF_HEAD_DIM) if m == 0 else (lane >= DIFF_HEAD_DIM)
                qd_ref[row:row + n_tok, g * LANES:(g + 1) * LANES] = jnp.where(
                    keep, x, jnp.zeros_like(x)).astype(qd_ref.dtype)


def _finish_diff(o_ref, acc_ref, l_ref, lam, g_sub, lam_init, n_tok):
    for g in range(DIFF_KV_HEADS):
        for r in range(DIFF_REP):
            r0 = ((g * DIFF_REP + r) * 2) * n_tok
            r1 = r0 + n_tok
            o0 = acc_ref[r0:r0 + n_tok, :] / l_ref[r0:r0 + n_tok, :]
            o1 = acc_ref[r1:r1 + n_tok, :] / l_ref[r1:r1 + n_tok, :]
            d = _rms(o0 - lam * o1, g_sub, SUBLN_EPS) * (1.0 - lam_init)
            col = (g * DIFF_REP + r) * LANES
            o_ref[:, col:col + LANES] = d.astype(o_ref.dtype)


def _diff_attn_kernel(dq_ref, dkb_ref, dvb_ref, dkmeta_ref, dvmeta_ref,
                      lq1_ref, lk1_ref, lq2_ref, lk2_ref, g_sub_ref, o_ref,
                      qd_ref, m_ref, l_ref, acc_ref, *, bq, bk, lam_init):
    qi = pl.program_id(1)
    half = DIFF_ROWS // DIFF_KV_HEADS * bq
    _fill_diff_queries(qd_ref, dq_ref, bq)
    qd = qd_ref[...]

    def mix(p, v):
        pb = p.astype(BF16)
        return jnp.concatenate(
            [_dot(pb[g * half:(g + 1) * half], v[:, g * LANES:(g + 1) * LANES])
             for g in range(DIFF_KV_HEADS)], axis=0)

    s = _dot_nt(qd, dkmeta_ref[...])
    m0 = jnp.max(s, axis=-1, keepdims=True)
    p = jnp.exp(s - m0)
    m_ref[...] = m0
    l_ref[...] = jnp.sum(p, axis=-1, keepdims=True)
    acc_ref[...] = mix(p, dvmeta_ref[...])

    def attend(start, masked):
        s = _dot_nt(qd, dkb_ref[pl.ds(start, bk), :])
        if masked:
            s = _causal(s, bq, qi * bq, start)
        m_prev = m_ref[...]
        m_new = jnp.maximum(m_prev, jnp.max(s, axis=-1, keepdims=True))
        alpha = jnp.exp(m_prev - m_new)
        p = jnp.exp(s - m_new)
        l_ref[...] = alpha * l_ref[...] + jnp.sum(p, axis=-1, keepdims=True)
        acc_ref[...] = alpha * acc_ref[...] + mix(p, dvb_ref[pl.ds(start, bk), :])
        m_ref[...] = m_new

    n_full = (qi * bq) // bk

    def body(kb, carry):
        attend(pl.multiple_of(kb * bk, bk), False)
        return carry

    lax.fori_loop(0, n_full, body, 0)
    attend(pl.multiple_of(n_full * bk, bk), True)

    lam = _lambda(lq1_ref, lk1_ref, lq2_ref, lk2_ref, lam_init)
    _finish_diff(o_ref, acc_ref, l_ref, lam, g_sub_ref[...], lam_init, bq)


def _diff_attn(dq, dkb, dvb, dkmeta, dvmeta, lams, g_sub, batch, seq, bq, bk, lam_init):
    nq = seq // bq
    rows = DIFF_ROWS * bq
    qrow = lambda b, i: (b * nq + i, 0)
    kv = lambda b, i: (b, 0)
    return pl.pallas_call(
        functools.partial(_diff_attn_kernel, bq=bq, bk=bk, lam_init=lam_init),
        grid=(batch, nq),
        in_specs=[pl.BlockSpec((bq, DIFF_WIDTH), qrow),
                  pl.BlockSpec((seq, DIFF_KV_WIDTH), kv),
                  pl.BlockSpec((seq, DIFF_KV_WIDTH), kv),
                  _const_spec(dkmeta.shape),
                  _const_spec(dvmeta.shape)]
                 + [_const_spec((1, DIFF_HEAD_DIM))] * 4
                 + [_const_spec((1, 2 * DIFF_HEAD_DIM))],
        out_specs=pl.BlockSpec((bq, DIFF_WIDTH), qrow),
        out_shape=jax.ShapeDtypeStruct((batch * seq, DIFF_WIDTH), BF16),
        scratch_shapes=[pltpu.VMEM((rows, DIFF_KV_WIDTH), BF16),
                        pltpu.VMEM((rows, 1), F32),
                        pltpu.VMEM((rows, 1), F32),
                        pltpu.VMEM((rows, 2 * DIFF_HEAD_DIM), F32)],
        compiler_params=pltpu.CompilerParams(
            dimension_semantics=("arbitrary", "arbitrary"), vmem_limit_bytes=VMEM_LIMIT),
        name="diff_attn",
    )(dq, dkb, dvb, dkmeta, dvmeta, *lams, g_sub)


def _paged_attn_kernel(pt_ref, qabs_ref, qpe_ref, dq_ref, c_new_ref, kr_new_ref, dk_new_ref,
                       dv_new_ref, lq1_ref, lk1_ref, lq2_ref, lk2_ref, g_sub_ref, *rest,
                       n_tok, pages, lam_init):
    del pt_ref
    page_refs = rest[:4 * pages]
    mla_o_ref, diff_o_ref = rest[4 * pages:4 * pages + 2]
    (qm_ref, qp_ref, qd_ref, kc_ref, kr_ref, kdk_ref, kdv_ref, new_c_ref, new_kr_ref,
     new_dk_ref, new_dv_ref, mm_ref, lm_ref, accm_ref, md_ref, ld_ref,
     accd_ref) = rest[4 * pages + 2:]
    j = pl.program_id(1)
    half = DIFF_ROWS // DIFF_KV_HEADS * n_tok

    def attend(kc, kr, kdk, kdv, masked):
        s_m = (_dot_nt(qm_ref[...].astype(BF16), kc)
               + _dot_nt(qp_ref[...].astype(BF16), kr)) * MLA_SCALE
        s_d = _dot_nt(qd_ref[...].astype(BF16), kdk)
        if masked:
            s_m = _causal(s_m, n_tok, 0, 0)
            s_d = _causal(s_d, n_tok, 0, 0)
        m_prev = mm_ref[...]
        m_new = jnp.maximum(m_prev, jnp.max(s_m, axis=-1, keepdims=True))
        alpha = jnp.exp(m_prev - m_new)
        p = jnp.exp(s_m - m_new)
        lm_ref[...] = alpha * lm_ref[...] + jnp.sum(p, axis=-1, keepdims=True)
        accm_ref[...] = alpha * accm_ref[...] + _dot(p.astype(BF16), kc)
        mm_ref[...] = m_new
        m_prev = md_ref[...]
        m_new = jnp.maximum(m_prev, jnp.max(s_d, axis=-1, keepdims=True))
        alpha = jnp.exp(m_prev - m_new)
        p = jnp.exp(s_d - m_new)
        ld_ref[...] = alpha * ld_ref[...] + jnp.sum(p, axis=-1, keepdims=True)
        p = p.astype(BF16)
        for g in range(DIFF_KV_HEADS):
            rows = slice(g * half, (g + 1) * half)
            accd_ref[rows, :] = alpha[rows] * accd_ref[rows, :] + _dot(
                p[rows], kdv[:, g * LANES:(g + 1) * LANES])
        md_ref[...] = m_new

    @pl.when(j == 0)
    def _():
        for h in range(MLA_HEADS):
            qm_ref[h * n_tok:(h + 1) * n_tok, :] = qabs_ref[:, h * MLA_KV_RANK:(h + 1) * MLA_KV_RANK]
            qp_ref[h * n_tok:(h + 1) * n_tok, :] = qpe_ref[:, h * MLA_ROPE:(h + 1) * MLA_ROPE]
        _fill_diff_queries(qd_ref, dq_ref, n_tok)
        mm_ref[...] = jnp.full(mm_ref.shape, NEG_INF, F32)
        lm_ref[...] = jnp.zeros(lm_ref.shape, F32)
        accm_ref[...] = jnp.zeros(accm_ref.shape, F32)
        md_ref[...] = jnp.full(md_ref.shape, NEG_INF, F32)
        ld_ref[...] = jnp.zeros(ld_ref.shape, F32)
        accd_ref[...] = jnp.zeros(accd_ref.shape, F32)
        for dst, src in ((new_c_ref, c_new_ref), (new_kr_ref, kr_new_ref),
                         (new_dk_ref, dk_new_ref), (new_dv_ref, dv_new_ref)):
            dst[...] = jnp.zeros(dst.shape, F32)
            dst[0:n_tok, :] = src[...]
        attend(new_c_ref[...].astype(BF16), new_kr_ref[...].astype(BF16),
               new_dk_ref[...].astype(BF16), new_dv_ref[...].astype(BF16), True)

    for p_ in range(pages):
        rows = slice(p_ * PAGE_SIZE, (p_ + 1) * PAGE_SIZE)
        kc_ref[rows, :] = page_refs[4 * p_][...].astype(BF16)
        kr_ref[rows, :] = page_refs[4 * p_ + 1][...].astype(BF16)
        kdk_ref[rows, :] = page_refs[4 * p_ + 2][...].astype(BF16)
        kdv_ref[rows, :] = page_refs[4 * p_ + 3][...].astype(BF16)
    attend(kc_ref[...], kr_ref[...], kdk_ref[...], kdv_ref[...], False)

    @pl.when(j == pl.num_programs(1) - 1)
    def _():
        for h in range(MLA_HEADS):
            rows = slice(h * n_tok, (h + 1) * n_tok)
            mla_o_ref[:, h * MLA_KV_RANK:(h + 1) * MLA_KV_RANK] = accm_ref[rows, :] / lm_ref[rows, :]
        lam = _lambda(lq1_ref, lk1_ref, lq2_ref, lk2_ref, lam_init)
        _finish_diff(diff_o_ref, accd_ref, ld_ref, lam, g_sub_ref[...], lam_init, n_tok)


def _paged_attn(page_table, qabs, qpe, dq, c_new, kr_new, dk_new, dv_new, lams, g_sub,
                pools, n_tok, pages, lam_init):
    n_seq, n_pages = page_table.shape
    seq_row = lambda b, j, pt: (b, 0)
    const = lambda b, j, pt: (0, 0)
    page_specs = []
    page_args = []
    for p_ in range(pages):
        for pool in pools:
            page_specs.append(pl.BlockSpec(
                (None, None, PAGE_SIZE, pool.shape[-1]),
                functools.partial(lambda b, j, pt, p_: (0, pt[b, j * pages + p_], 0, 0), p_=p_)))
            page_args.append(pool)
    n_keys = pages * PAGE_SIZE
    mla_rows = MLA_HEADS * n_tok
    diff_rows = DIFF_ROWS * n_tok
    grid_spec = pltpu.PrefetchScalarGridSpec(
        num_scalar_prefetch=1,
        grid=(n_seq, n_pages // pages),
        in_specs=[pl.BlockSpec((n_tok, MLA_HEADS * MLA_KV_RANK), seq_row),
                  pl.BlockSpec((n_tok, MLA_HEADS * MLA_ROPE), seq_row),
                  pl.BlockSpec((n_tok, DIFF_WIDTH), seq_row),
                  pl.BlockSpec((n_tok, MLA_KV_RANK), seq_row),
                  pl.BlockSpec((n_tok, MLA_ROPE), seq_row),
                  pl.BlockSpec((n_tok, DIFF_KV_WIDTH), seq_row),
                  pl.BlockSpec((n_tok, DIFF_KV_WIDTH), seq_row)]
                 + [pl.BlockSpec((1, DIFF_HEAD_DIM), const)] * 4
                 + [pl.BlockSpec((1, 2 * DIFF_HEAD_DIM), const)]
                 + page_specs,
        out_specs=[pl.BlockSpec((n_tok, MLA_HEADS * MLA_KV_RANK), seq_row),
                   pl.BlockSpec((n_tok, DIFF_WIDTH), seq_row)],
        scratch_shapes=[pltpu.VMEM((mla_rows, MLA_KV_RANK), F32),
                        pltpu.VMEM((mla_rows, MLA_ROPE), F32),
                        pltpu.VMEM((diff_rows, DIFF_KV_WIDTH), F32),
                        pltpu.VMEM((n_keys, MLA_KV_RANK), BF16),
                        pltpu.VMEM((n_keys, MLA_ROPE), BF16),
                        pltpu.VMEM((n_keys, DIFF_KV_WIDTH), BF16),
                        pltpu.VMEM((n_keys, DIFF_KV_WIDTH), BF16),
                        pltpu.VMEM((PAGE_SIZE, MLA_KV_RANK), F32),
                        pltpu.VMEM((PAGE_SIZE, MLA_ROPE), F32),
                        pltpu.VMEM((PAGE_SIZE, DIFF_KV_WIDTH), F32),
                        pltpu.VMEM((PAGE_SIZE, DIFF_KV_WIDTH), F32),
                        pltpu.VMEM((mla_rows, 1), F32),
                        pltpu.VMEM((mla_rows, 1), F32),
                        pltpu.VMEM((mla_rows, MLA_KV_RANK), F32),
                        pltpu.VMEM((diff_rows, 1), F32),
                        pltpu.VMEM((diff_rows, 1), F32),
                        pltpu.VMEM((diff_rows, 2 * DIFF_HEAD_DIM), F32)],
    )
    rows = n_seq * n_tok
    return pl.pallas_call(
        functools.partial(_paged_attn_kernel, n_tok=n_tok, pages=pages, lam_init=lam_init),
        grid_spec=grid_spec,
        out_shape=[jax.ShapeDtypeStruct((rows, MLA_HEADS * MLA_KV_RANK), F32),
                   jax.ShapeDtypeStruct((rows, DIFF_WIDTH), F32)],
        compiler_params=pltpu.CompilerParams(
            dimension_semantics=("arbitrary", "arbitrary"), vmem_limit_bytes=VMEM_LIMIT),
        name="paged_attn",
    )(page_table, qabs, qpe, dq, c_new, kr_new, dk_new, dv_new, *lams, g_sub, *page_args)


def _out_proj_kernel(x_ref, lat_ref, diff_ref, w_uv_ref, g_mla_ref, w_out_ref, h_ref, mla_ref):
    for h in range(MLA_HEADS):
        lat = lat_ref[:, h * MLA_KV_RANK:(h + 1) * MLA_KV_RANK].astype(BF16)
        mla_ref[:, h * MLA_V:(h + 1) * MLA_V] = _dot(lat, w_uv_ref[h])
    mla = _rms(mla_ref[...], g_mla_ref[...], NORM_EPS).astype(BF16)
    h_ref[...] = (x_ref[...] + _dot(mla, w_out_ref[0:MLA_WIDTH, :])
                  + _dot(diff_ref[...].astype(BF16), w_out_ref[MLA_WIDTH:MLA_WIDTH + DIFF_WIDTH, :]))


def _out_proj(x, lat, diff, w, tm):
    rows, d = x.shape
    row = lambda i: (i, 0)
    return pl.pallas_call(
        _out_proj_kernel,
        grid=(rows // tm,),
        in_specs=[pl.BlockSpec((tm, d), row),
                  pl.BlockSpec((tm, lat.shape[1]), row),
                  pl.BlockSpec((tm, diff.shape[1]), row),
                  _const_spec(w["w_uv"].shape),
                  _const_spec((1, MLA_WIDTH)),
                  _const_spec(w["w_out"].shape)],
        out_specs=pl.BlockSpec((tm, d), row),
        out_shape=jax.ShapeDtypeStruct((rows, d), F32),
        scratch_shapes=[pltpu.VMEM((tm, MLA_WIDTH), F32)],
        compiler_params=pltpu.CompilerParams(
            dimension_semantics=("arbitrary",), vmem_limit_bytes=VMEM_LIMIT),
        name="out_proj",
    )(x, lat, diff, w["w_uv"], w["g_mla_out"], w["w_out"])


def _ffn_kernel(h_ref, g_ffn_ref, w_gate_ref, w_up_ref, w_down_ref, g_final_ref, o_ref, xn_ref):
    j = pl.program_id(1)

    @pl.when(j == 0)
    def _():
        h = h_ref[...]
        xn_ref[...] = _rms(h, g_ffn_ref[...], NORM_EPS).astype(BF16)
        o_ref[...] = h

    xn = xn_ref[...]
    gate = _dot(xn, w_gate_ref[...])
    up = _dot(xn, w_up_ref[...])
    act = (gate * (1.0 / (1.0 + jnp.exp(-gate))) * up).astype(BF16)
    o_ref[...] += _dot(act, w_down_ref[...])

    @pl.when(j == pl.num_programs(1) - 1)
    def _():
        o_ref[...] = _rms(o_ref[...], g_final_ref[...], NORM_EPS)


def _ffn(h, w, tm, th):
    rows, d = h.shape
    hidden = w["w_gate"].shape[1]
    row = lambda i, j: (i, 0)
    return pl.pallas_call(
        _ffn_kernel,
        grid=(rows // tm, hidden // th),
        in_specs=[pl.BlockSpec((tm, d), row),
                  pl.BlockSpec((1, d), lambda i, j: (0, 0)),
                  pl.BlockSpec((d, th), lambda i, j: (0, j)),
                  pl.BlockSpec((d, th), lambda i, j: (0, j)),
                  pl.BlockSpec((th, d), lambda i, j: (j, 0)),
                  pl.BlockSpec((1, d), lambda i, j: (0, 0))],
        out_specs=pl.BlockSpec((tm, d), row),
        out_shape=jax.ShapeDtypeStruct((rows, d), F32),
        scratch_shapes=[pltpu.VMEM((tm, d), BF16)],
        compiler_params=pltpu.CompilerParams(
            dimension_semantics=("arbitrary", "arbitrary"), vmem_limit_bytes=VMEM_LIMIT),
        name="ffn",
    )(h, w["g_ffn"], w["w_gate"], w["w_up"], w["w_down"], w["g_final"])


def _rope_tables(pos, rot_dim):
    half = rot_dim // 2
    inv = ROPE_THETA ** (-jnp.arange(half, dtype=F32) * 2.0 / rot_dim)
    ang = pos.astype(F32)[:, None] * inv
    d = jnp.arange(LANES) % DIFF_HEAD_DIM
    a = ang[:, d % half]
    cos = jnp.where(d < rot_dim, jnp.cos(a), 1.0)
    sin = jnp.sin(a)
    sin_lo = jnp.where(d < half, -sin, 0.0)
    sin_hi = jnp.where((d >= half) & (d < rot_dim), sin, 0.0)
    return cos.astype(F32), sin_lo.astype(F32), sin_hi.astype(F32)


def _tables(pos):
    return _rope_tables(pos, MLA_ROPE) + _rope_tables(pos, DIFF_ROT)


def _row_block(rows, target):
    tm = min(rows, target)
    while rows % tm:
        tm //= 2
    return tm


def kernel(x_prompt, x_sample, cache_mla_latent, cache_mla_krope, cache_diff_k, cache_diff_v,
           page_table, meta_tokens, g_attn, w_in, g_q_a, w_q_b, g_kv_a, w_uk, w_uv, g_mla_out,
           lambda_q1, lambda_k1, lambda_q2, lambda_k2, g_subln, w_out, g_ffn, w_gate, w_up,
           w_down, g_final):
    batch, seq, d = x_prompt.shape
    n_seq, n_tok, _ = x_sample.shape
    depth = w_in.shape[0]
    assert depth == 1, "single-layer trunk"
    n_phys = cache_mla_latent.shape[1]
    past_len = page_table.shape[1] * PAGE_SIZE
    lam_init = 0.8 - 0.6 * math.exp(-0.3 * 0)

    wi = w_in[0]
    o = [0, MLA_Q_RANK, MLA_Q_RANK + MLA_KV_RANK, MLA_Q_RANK + MLA_KV_RANK + MLA_ROPE]
    o.append(o[3] + DIFF_WIDTH)
    o.append(o[4] + DIFF_KV_WIDTH)
    o.append(o[5] + DIFF_KV_WIDTH)
    w_in_r = jnp.concatenate(
        [wi[:, o[0]:o[2]], wi[:, o[3]:o[6]], wi[:, o[2]:o[3]],
         jnp.zeros((d, LANES - MLA_ROPE), wi.dtype)], axis=1).astype(BF16)
    wq = w_q_b[0]
    w_qb = jnp.concatenate(
        [wq[:, :, :MLA_QK_NOPE].reshape(MLA_Q_RANK, -1),
         wq[:, :, MLA_QK_NOPE:].reshape(MLA_Q_RANK, -1)], axis=1).astype(BF16)
    w = {
        "g_attn": g_attn[0][None], "w_in": w_in_r, "g_q_a": g_q_a[0][None],
        "g_kv_a": g_kv_a[0][None], "w_qb": w_qb,
        "w_uk": jnp.transpose(w_uk[0], (1, 2, 0)).astype(BF16),
        "w_uv": jnp.transpose(w_uv[0], (1, 0, 2)).astype(BF16),
        "g_mla_out": g_mla_out[0][None], "w_out": w_out[0].astype(BF16),
        "g_ffn": g_ffn[0][None], "w_gate": w_gate[0].astype(BF16),
        "w_up": w_up[0].astype(BF16), "w_down": w_down[0].astype(BF16),
        "g_final": g_final[None],
    }
    lams = (lambda_q1, lambda_k1, lambda_q2, lambda_k2)
    g_sub = g_subln[0][None]

    xp = x_prompt.reshape(batch * seq, d)
    xs = x_sample.reshape(n_seq * n_tok, d)
    tm_p = _row_block(seq, 256)
    tm_s = _row_block(n_seq * n_tok, 256)
    tab_p = _tables(N_META + jnp.arange(seq))
    tab_s = _tables(jnp.tile(past_len + jnp.arange(n_tok), tm_s // n_tok))
    tab_m = _tables(jnp.arange(N_META))
    (qabs_p, qpe_p, dq_p, c_p, kr_p, dk_p, dv_p, cb_p, krb_p, dkb_p, dvb_p) = _proj(
        xp, tab_p, tm_p, BF16, w)
    (qabs_s, qpe_s, dq_s, c_s, kr_s, dk_s, dv_s, _, _, _, _) = _proj(xs, tab_s, tm_s, F32, w)
    (_, _, _, c_m, kr_m, dk_m, dv_m, cb_m, krb_m, dkb_m, dvb_m) = _proj(
        meta_tokens.astype(F32), tab_m, N_META, BF16, w)

    lat_p = _mla_attn(qabs_p, qpe_p, cb_p, krb_p, cb_m, krb_m, batch, seq, _row_block(seq, 256))
    diff_p = _diff_attn(dq_p, dkb_p, dvb_p, dkb_m, dvb_m, lams, g_sub, batch, seq,
                        _row_block(seq, 128), _row_block(seq, 256), lam_init)

    pools = (cache_mla_latent, cache_mla_krope,
             cache_diff_k.reshape(depth, n_phys, PAGE_SIZE, DIFF_KV_WIDTH),
             cache_diff_v.reshape(depth, n_phys, PAGE_SIZE, DIFF_KV_WIDTH))
    pages = _row_block(page_table.shape[1], 8)
    lat_s, diff_s = _paged_attn(page_table, qabs_s, qpe_s, dq_s, c_s, kr_s, dk_s, dv_s, lams,
                                g_sub, pools, n_tok, pages, lam_init)

    h_p = _out_proj(xp, lat_p, diff_p, w, _row_block(batch * seq, 512))
    h_s = _out_proj(xs, lat_s, diff_s, w, _row_block(n_seq * n_tok, 512))
    hidden = w_gate.shape[2]
    th = 512 if hidden % 512 == 0 else hidden
    y_p = _ffn(h_p, w, _row_block(batch * seq, 512), th).reshape(batch, seq, d)
    y_s = _ffn(h_s, w, _row_block(n_seq * n_tok, 512), th).reshape(n_seq, n_tok, d)

    def with_meta(meta, real):
        n = real.shape[-1]
        full = jnp.concatenate(
            [jnp.broadcast_to(meta[None], (batch, N_META, n)), real.reshape(batch, seq, n)], axis=1)
        return full[None]

    t = seq + N_META
    kv_shape = (DIFF_KV_HEADS, 2, DIFF_HEAD_DIM)
    v_shape = (DIFF_KV_HEADS, 2 * DIFF_HEAD_DIM)
    return (y_p, y_s,
            with_meta(c_m, c_p), with_meta(kr_m, kr_p),
            with_meta(dk_m, dk_p).reshape((1, batch, t) + kv_shape),
            with_meta(dv_m, dv_p).reshape((1, batch, t) + v_shape),
            c_s.reshape(1, n_seq, n_tok, -1), kr_s.reshape(1, n_seq, n_tok, -1),
            dk_s.reshape((1, n_seq, n_tok) + kv_shape),
            dv_s.reshape((1, n_seq, n_tok) + v_shape))
```

```python
import functools
import math

import jax
import jax.numpy as jnp
from jax import lax
from jax.experimental import pallas as pl
from jax.experimental.pallas import tpu as pltpu

F32 = jnp.float32
BF16 = jnp.bfloat16

N_META = 16
ROPE_THETA = 500000.0
NORM_EPS = 1e-6
SUBLN_EPS = 1e-5
NEG_INF = -1e30
PAGE_SIZE = 128

MLA_HEADS = 8
MLA_QK_NOPE = 128
MLA_ROPE = 64
MLA_V = 128
MLA_Q_RANK = 512
MLA_KV_RANK = 256
MLA_WIDTH = MLA_HEADS * MLA_V
MLA_SCALE = (MLA_QK_NOPE + MLA_ROPE) ** -0.5

DIFF_HEADS = 8
DIFF_KV_HEADS = 2
DIFF_REP = DIFF_HEADS // DIFF_KV_HEADS
DIFF_HEAD_DIM = 64
DIFF_ROT = DIFF_HEAD_DIM // 4
DIFF_WIDTH = DIFF_HEADS * 2 * DIFF_HEAD_DIM
DIFF_KV_WIDTH = DIFF_KV_HEADS * 2 * DIFF_HEAD_DIM
DIFF_SCALE = DIFF_HEAD_DIM ** -0.5
DIFF_ROWS = DIFF_KV_HEADS * DIFF_REP * 2

LANES = 128
VMEM_LIMIT = 56 * 1024 * 1024

OFF_QA = 0
OFF_KVA = OFF_QA + MLA_Q_RANK
OFF_DQ = OFF_KVA + MLA_KV_RANK
OFF_DK = OFF_DQ + DIFF_WIDTH
OFF_DV = OFF_DK + DIFF_KV_WIDTH
OFF_KR = OFF_DV + DIFF_KV_WIDTH
IN_WIDTH_PADDED = OFF_KR + LANES


def _rms(x, g, eps):
    return x * lax.rsqrt(jnp.mean(x * x, axis=-1, keepdims=True) + eps) * g


def _dot(a, b):
    return jnp.dot(a, b, preferred_element_type=F32)


def _dot_nt(a, b):
    return lax.dot_general(a, b, (((1,), (1,)), ((), ())), preferred_element_type=F32)


def _rope_chunk(x, cos, sin_lo, sin_hi, half):
    return (x * cos + pltpu.roll(x, LANES - half, 1) * sin_lo
            + pltpu.roll(x, half, 1) * sin_hi)


def _causal(s, n_tok, tok0, col0):
    assert n_tok & (n_tok - 1) == 0
    tok = tok0 + (lax.broadcasted_iota(jnp.int32, s.shape, 0) & (n_tok - 1))
    col = col0 + lax.broadcasted_iota(jnp.int32, s.shape, 1)
    return jnp.where(col <= tok, s, NEG_INF)


def _const_spec(shape):
    return pl.BlockSpec(shape, lambda *_: (0,) * len(shape))


def _proj_kernel(x_ref, g_attn_ref, w_in_ref, g_q_ref, g_kv_ref, w_qb_ref, w_uk_ref,
                 cos_m_ref, slo_m_ref, shi_m_ref, cos_d_ref, slo_d_ref, shi_d_ref,
                 qabs_ref, qpe_ref, dq_ref, c_ref, kr_ref, dk_ref, dv_ref,
                 cb_ref, krb_ref, dkb_ref, dvb_ref):
    xn = _rms(x_ref[...], g_attn_ref[...], NORM_EPS).astype(BF16)
    z = _dot(xn, w_in_ref[...])

    cos_m, slo_m, shi_m = cos_m_ref[...], slo_m_ref[...], shi_m_ref[...]
    cos_d, slo_d, shi_d = cos_d_ref[...], slo_d_ref[...], shi_d_ref[...]

    c = _rms(z[:, OFF_KVA:OFF_KVA + MLA_KV_RANK], g_kv_ref[...], NORM_EPS)
    c_ref[...] = c
    cb_ref[...] = c.astype(BF16)
    kr = _rope_chunk(z[:, OFF_KR:OFF_KR + LANES], cos_m, slo_m, shi_m, MLA_ROPE // 2)
    kr_ref[...] = kr[:, :MLA_ROPE]
    krb_ref[...] = kr[:, :MLA_ROPE].astype(BF16)

    qn = _rms(z[:, OFF_QA:OFF_QA + MLA_Q_RANK], g_q_ref[...], NORM_EPS).astype(BF16)
    q = _dot(qn, w_qb_ref[...])
    nope_w = MLA_HEADS * MLA_QK_NOPE
    for h in range(MLA_HEADS):
        qh = q[:, h * MLA_QK_NOPE:(h + 1) * MLA_QK_NOPE].astype(BF16)
        qabs_ref[:, h * MLA_KV_RANK:(h + 1) * MLA_KV_RANK] = _dot(qh, w_uk_ref[h]).astype(qabs_ref.dtype)
    for j in range(MLA_HEADS * MLA_ROPE // LANES):
        qc = q[:, nope_w + j * LANES:nope_w + (j + 1) * LANES]
        qpe_ref[:, j * LANES:(j + 1) * LANES] = _rope_chunk(
            qc, cos_m, slo_m, shi_m, MLA_ROPE // 2).astype(qpe_ref.dtype)

    for j in range(DIFF_WIDTH // LANES):
        xc = z[:, OFF_DQ + j * LANES:OFF_DQ + (j + 1) * LANES]
        dq_ref[:, j * LANES:(j + 1) * LANES] = _rope_chunk(
            xc, cos_d, slo_d, shi_d, DIFF_ROT // 2).astype(dq_ref.dtype)
    for j in range(DIFF_KV_WIDTH // LANES):
        xc = z[:, OFF_DK + j * LANES:OFF_DK + (j + 1) * LANES]
        dk = _rope_chunk(xc, cos_d, slo_d, shi_d, DIFF_ROT // 2)
        dk_ref[:, j * LANES:(j + 1) * LANES] = dk
        dkb_ref[:, j * LANES:(j + 1) * LANES] = dk.astype(BF16)
    dv = z[:, OFF_DV:OFF_DV + DIFF_KV_WIDTH]
    dv_ref[...] = dv
    dvb_ref[...] = dv.astype(BF16)


def _proj(x, tables, tm, q_dtype, w):
    rows, d = x.shape
    t_tab = tables[0].shape[0]
    n_tab = t_tab // tm
    row = lambda i: (i, 0)
    tab = lambda i: (i % n_tab, 0)
    out_widths = [(MLA_HEADS * MLA_KV_RANK, q_dtype), (MLA_HEADS * MLA_ROPE, q_dtype),
                  (DIFF_WIDTH, q_dtype), (MLA_KV_RANK, F32), (MLA_ROPE, F32),
                  (DIFF_KV_WIDTH, F32), (DIFF_KV_WIDTH, F32), (MLA_KV_RANK, BF16),
                  (MLA_ROPE, BF16), (DIFF_KV_WIDTH, BF16), (DIFF_KV_WIDTH, BF16)]
    return pl.pallas_call(
        _proj_kernel,
        grid=(rows // tm,),
        in_specs=[pl.BlockSpec((tm, d), row),
                  _const_spec((1, d)),
                  _const_spec(w["w_in"].shape),
                  _const_spec((1, MLA_Q_RANK)),
                  _const_spec((1, MLA_KV_RANK)),
                  _const_spec(w["w_qb"].shape),
                  _const_spec(w["w_uk"].shape)]
                 + [pl.BlockSpec((tm, LANES), tab)] * 6,
        out_specs=[pl.BlockSpec((tm, n), row) for n, _ in out_widths],
        out_shape=[jax.ShapeDtypeStruct((rows, n), dt) for n, dt in out_widths],
        compiler_params=pltpu.CompilerParams(
            dimension_semantics=("arbitrary",), vmem_limit_bytes=VMEM_LIMIT),
        name="proj",
    )(x, w["g_attn"], w["w_in"], w["g_q_a"], w["g_kv_a"], w["w_qb"], w["w_uk"], *tables)


def _t(x):
    return x.astype(F32).T


def _online_softmax_t(s, m_prev, l_prev, log2_scale):
    m_new = jnp.maximum(m_prev, jnp.max(s, axis=0, keepdims=True))
    alpha = jnp.exp2((m_prev - m_new) * log2_scale)
    p = jnp.exp2((s - m_new) * log2_scale)
    return p, alpha, m_new, alpha * l_prev + jnp.sum(p, axis=0, keepdims=True)


def _mla_attn_kernel(qabs_ref, qpe_ref, cb_ref, krb_ref, cmeta_ref, krmeta_ref, o_ref,
                     qt_ref, ct_ref, cm_ref, krm_ref, cmt_ref, m_ref, l_ref, acc_ref, *, bq):
    qi = pl.program_id(1)
    seq = cb_ref.shape[0]
    log2_scale = MLA_SCALE * math.log2(math.e)

    @pl.when(qi == 0)
    def _():
        for kb in range(seq // bq):
            ct_ref[kb] = _t(cb_ref[kb * bq:(kb + 1) * bq, :]).astype(BF16)
        cm_ref[...] = jnp.zeros(cm_ref.shape, BF16)
        cm_ref[0:N_META, :] = cmeta_ref[...]
        krm_ref[...] = jnp.zeros(krm_ref.shape, BF16)
        krm_ref[0:N_META, :] = krmeta_ref[...]
        cmt_ref[...] = _t(cm_ref[...]).astype(BF16)

    for h in range(MLA_HEADS):
        qt_ref[h, 0:MLA_KV_RANK, :] = _t(
            qabs_ref[:, h * MLA_KV_RANK:(h + 1) * MLA_KV_RANK]).astype(BF16)
    per_chunk = LANES // MLA_ROPE
    for j in range(MLA_HEADS // per_chunk):
        t = _t(qpe_ref[:, j * LANES:(j + 1) * LANES]).astype(BF16)
        for i in range(per_chunk):
            qt_ref[j * per_chunk + i, MLA_KV_RANK:MLA_KV_RANK + MLA_ROPE, :] = (
                t[i * MLA_ROPE:(i + 1) * MLA_ROPE])
    m_ref[...] = jnp.full(m_ref.shape, NEG_INF, F32)
    l_ref[...] = jnp.zeros(l_ref.shape, F32)
    acc_ref[...] = jnp.zeros(acc_ref.shape, F32)

    def attend(kc, kkr, ct, visible):
        for h in range(MLA_HEADS):
            s = (_dot(kc, qt_ref[h, 0:MLA_KV_RANK, :])
                 + _dot(kkr, qt_ref[h, MLA_KV_RANK:MLA_KV_RANK + MLA_ROPE, :]))
            if visible is not None:
                s = jnp.where(visible, s, NEG_INF)
            p, alpha, m_new, l_new = _online_softmax_t(
                s, m_ref[h:h + 1, :], l_ref[h:h + 1, :], log2_scale)
            acc_ref[h] = alpha * acc_ref[h] + _dot(ct, p.astype(BF16))
            m_ref[h:h + 1, :] = m_new
            l_ref[h:h + 1, :] = l_new

    key = lax.broadcasted_iota(jnp.int32, (LANES, bq), 0)
    attend(cm_ref[...], krm_ref[...], cmt_ref[...], key < N_META)

    def body(kb, carry):
        start = pl.multiple_of(kb * bq, bq)
        attend(cb_ref[pl.ds(start, bq), :], krb_ref[pl.ds(start, bq), :], ct_ref[kb], None)
        return carry

    lax.fori_loop(0, qi, body, 0)
    start = pl.multiple_of(qi * bq, bq)
    key = lax.broadcasted_iota(jnp.int32, (bq, bq), 0)
    tok = lax.broadcasted_iota(jnp.int32, (bq, bq), 1)
    attend(cb_ref[pl.ds(start, bq), :], krb_ref[pl.ds(start, bq), :], ct_ref[qi], key <= tok)

    for h in range(MLA_HEADS):
        o_ref[:, h * MLA_KV_RANK:(h + 1) * MLA_KV_RANK] = (
            acc_ref[h] / l_ref[h:h + 1, :]).T.astype(o_ref.dtype)


def _mla_attn(qabs, qpe, cb, krb, cmeta, krmeta, batch, seq, bq):
    nq = seq // bq
    qrow = lambda b, i: (b * nq + i, 0)
    kv = lambda b, i: (b, 0)
    return pl.pallas_call(
        functools.partial(_mla_attn_kernel, bq=bq),
        grid=(batch, nq),
        in_specs=[pl.BlockSpec((bq, MLA_HEADS * MLA_KV_RANK), qrow),
                  pl.BlockSpec((bq, MLA_HEADS * MLA_ROPE), qrow),
                  pl.BlockSpec((seq, MLA_KV_RANK), kv),
                  pl.BlockSpec((seq, MLA_ROPE), kv),
                  _const_spec(cmeta.shape),
                  _const_spec(krmeta.shape)],
        out_specs=pl.BlockSpec((bq, MLA_HEADS * MLA_KV_RANK), qrow),
        out_shape=jax.ShapeDtypeStruct((batch * seq, MLA_HEADS * MLA_KV_RANK), BF16),
        scratch_shapes=[pltpu.VMEM((MLA_HEADS, MLA_KV_RANK + MLA_ROPE, bq), BF16),
                        pltpu.VMEM((nq, MLA_KV_RANK, bq), BF16),
                        pltpu.VMEM((LANES, MLA_KV_RANK), BF16),
                        pltpu.VMEM((LANES, MLA_ROPE), BF16),
                        pltpu.VMEM((MLA_KV_RANK, LANES), BF16),
                        pltpu.VMEM((MLA_HEADS, bq), F32),
                        pltpu.VMEM((MLA_HEADS, bq), F32),
                        pltpu.VMEM((MLA_HEADS, MLA_KV_RANK, bq), F32)],
        compiler_params=pltpu.CompilerParams(
            dimension_semantics=("arbitrary", "arbitrary"), vmem_limit_bytes=VMEM_LIMIT),
        name="mla_attn",
    )(qabs, qpe, cb, krb, cmeta, krmeta)


def _lambda(lq1_ref, lk1_ref, lq2_ref, lk2_ref, lam_init):
    a = jnp.sum(lq1_ref[...] * lk1_ref[...], axis=-1, keepdims=True)
    b = jnp.sum(lq2_ref[...] * lk2_ref[...], axis=-1, keepdims=True)
    return jnp.exp(a) - jnp.exp(b) + lam_init


def _fill_diff_queries(qd_ref, dq_ref, n_tok):
    qd_ref[...] = jnp.zeros(qd_ref.shape, qd_ref.dtype)
    lane = lax.broadcasted_iota(jnp.int32, (n_tok, LANES), 1)
    for g in range(DIFF_KV_HEADS):
        for r in range(DIFF_REP):
            col = (g * DIFF_REP + r) * LANES
            x = dq_ref[:, col:col + LANES] * DIFF_SCALE
            for m in range(2):
                row = ((g * DIFF_REP + r) * 2 + m) * n_tok
                keep = (lane < DIFF_HEAD_DIM) if m == 0 else (lane >= DIFF_HEAD_DIM)
                qd_ref[row:row + n_tok, g * LANES:(g + 1) * LANES] = jnp.where(
                    keep, x, jnp.zeros_like(x)).astype(qd_ref.dtype)


def _finish_diff(o_ref, acc_ref, l_ref, lam, g_sub, lam_init, n_tok):
    for g in range(DIFF_KV_HEADS):
        for r in range(DIFF_REP):
            r0 = ((g * DIFF_REP + r) * 2) * n_tok
            r1 = r0 + n_tok
            o0 = acc_ref[r0:r0 + n_tok, :] / l_ref[r0:r0 + n_tok, :]
            o1 = acc_ref[r1:r1 + n_tok, :] / l_ref[r1:r1 + n_tok, :]
            d = _rms(o0 - lam * o1, g_sub, SUBLN_EPS) * (1.0 - lam_init)
            col = (g * DIFF_REP + r) * LANES
            o_ref[:, col:col + LANES] = d.astype(o_ref.dtype)


def _diff_attn_kernel(dq_ref, dkb_ref, dvb_ref, dkmeta_ref, dvmeta_ref,
                      lq1_ref, lk1_ref, lq2_ref, lk2_ref, g_sub_ref, o_ref,
                      qt_ref, vt_ref, km_ref, vm_ref, vmt_ref, m_ref, l_ref, acc_ref,
                      *, bq, bk, lam_init):
    qi = pl.program_id(1)
    seq = dkb_ref.shape[0]
    n_chunks = DIFF_KV_HEADS * DIFF_REP
    log2e = math.log2(math.e)

    @pl.when(qi == 0)
    def _():
        for kb in range(seq // bk):
            vt_ref[kb] = _t(dvb_ref[kb * bk:(kb + 1) * bk, :]).astype(BF16)
        km_ref[...] = jnp.zeros(km_ref.shape, BF16)
        km_ref[0:N_META, :] = dkmeta_ref[...]
        vm_ref[...] = jnp.zeros(vm_ref.shape, BF16)
        vm_ref[0:N_META, :] = dvmeta_ref[...]
        vmt_ref[...] = _t(vm_ref[...]).astype(BF16)

    dim = lax.broadcasted_iota(jnp.int32, (LANES, bq), 0)
    for c in range(n_chunks):
        x = _t(dq_ref[:, c * LANES:(c + 1) * LANES] * DIFF_SCALE)
        qt_ref[c, :, 0:bq] = jnp.where(dim < DIFF_HEAD_DIM, x, 0.0).astype(BF16)
        qt_ref[c, :, bq:2 * bq] = jnp.where(dim >= DIFF_HEAD_DIM, x, 0.0).astype(BF16)
    m_ref[...] = jnp.full(m_ref.shape, NEG_INF, F32)
    l_ref[...] = jnp.zeros(l_ref.shape, F32)
    acc_ref[...] = jnp.zeros(acc_ref.shape, F32)

    def attend(dk, vt, visible):
        for c in range(n_chunks):
            g = c // DIFF_REP
            s = _dot(dk[:, g * LANES:(g + 1) * LANES], qt_ref[c])
            if visible is not None:
                s = jnp.where(visible, s, NEG_INF)
            p, alpha, m_new, l_new = _online_softmax_t(
                s, m_ref[c:c + 1, :], l_ref[c:c + 1, :], log2e)
            acc_ref[c] = alpha * acc_ref[c] + _dot(vt[g * LANES:(g + 1) * LANES, :], p.astype(BF16))
            m_ref[c:c + 1, :] = m_new
            l_ref[c:c + 1, :] = l_new

    key = lax.broadcasted_iota(jnp.int32, (LANES, 2 * bq), 0)
    attend(km_ref[...], vmt_ref[...], key < N_META)

    n_full = (qi * bq) // bk

    def body(kb, carry):
        start = pl.multiple_of(kb * bk, bk)
        attend(dkb_ref[pl.ds(start, bk), :], vt_ref[kb], None)
        return carry

    lax.fori_loop(0, n_full, body, 0)
    start = pl.multiple_of(n_full * bk, bk)
    key = start + lax.broadcasted_iota(jnp.int32, (bk, 2 * bq), 0)
    tok = qi * bq + (lax.broadcasted_iota(jnp.int32, (bk, 2 * bq), 1) & (bq - 1))
    attend(dkb_ref[pl.ds(start, bk), :], vt_ref[n_full], key <= tok)

    lam = _lambda(lq1_ref, lk1_ref, lq2_ref, lk2_ref, lam_init)
    for c in range(n_chunks):
        o = acc_ref[c] / l_ref[c:c + 1, :]
        d = (o[:, 0:bq] - lam * o[:, bq:2 * bq]).T
        d = _rms(d, g_sub_ref[...], SUBLN_EPS) * (1.0 - lam_init)
        o_ref[:, c * LANES:(c + 1) * LANES] = d.astype(o_ref.dtype)


def _diff_attn(dq, dkb, dvb, dkmeta, dvmeta, lams, g_sub, batch, seq, bq, bk, lam_init):
    assert bq & (bq - 1) == 0 and bk % bq == 0 and seq % bk == 0
    nq = seq // bq
    n_chunks = DIFF_KV_HEADS * DIFF_REP
    qrow = lambda b, i: (b * nq + i, 0)
    kv = lambda b, i: (b, 0)
    return pl.pallas_call(
        functools.partial(_diff_attn_kernel, bq=bq, bk=bk, lam_init=lam_init),
        grid=(batch, nq),
        in_specs=[pl.BlockSpec((bq, DIFF_WIDTH), qrow),
                  pl.BlockSpec((seq, DIFF_KV_WIDTH), kv),
                  pl.BlockSpec((seq, DIFF_KV_WIDTH), kv),
                  _const_spec(dkmeta.shape),
                  _const_spec(dvmeta.shape)]
                 + [_const_spec((1, DIFF_HEAD_DIM))] * 4
                 + [_const_spec((1, 2 * DIFF_HEAD_DIM))],
        out_specs=pl.BlockSpec((bq, DIFF_WIDTH), qrow),
        out_shape=jax.ShapeDtypeStruct((batch * seq, DIFF_WIDTH), BF16),
        scratch_shapes=[pltpu.VMEM((n_chunks, LANES, 2 * bq), BF16),
                        pltpu.VMEM((seq // bk, DIFF_KV_WIDTH, bk), BF16),
                        pltpu.VMEM((LANES, DIFF_KV_WIDTH), BF16),
                        pltpu.VMEM((LANES, DIFF_KV_WIDTH), BF16),
                        pltpu.VMEM((DIFF_KV_WIDTH, LANES), BF16),
                        pltpu.VMEM((n_chunks, 2 * bq), F32),
                        pltpu.VMEM((n_chunks, 2 * bq), F32),
                        pltpu.VMEM((n_chunks, 2 * DIFF_HEAD_DIM, 2 * bq), F32)],
        compiler_params=pltpu.CompilerParams(
            dimension_semantics=("arbitrary", "arbitrary"), vmem_limit_bytes=VMEM_LIMIT),
        name="diff_attn",
    )(dq, dkb, dvb, dkmeta, dvmeta, *lams, g_sub)


def _paged_attn_kernel(pt_ref, qabs_ref, qpe_ref, dq_ref, c_new_ref, kr_new_ref, dk_new_ref,
                       dv_new_ref, lq1_ref, lk1_ref, lq2_ref, lk2_ref, g_sub_ref, *rest,
                       n_tok, pages, lam_init):
    del pt_ref
    page_refs = rest[:4 * pages]
    mla_o_ref, diff_o_ref = rest[4 * pages:4 * pages + 2]
    (qm_ref, qp_ref, qd_ref, kc_ref, krt_ref, dkt_ref, dv_ref, new_c_ref, new_kr_ref,
     new_dk_ref, new_dv_ref, mm_ref, lm_ref, accm_ref, md_ref, ld_ref,
     accd_ref) = rest[4 * pages + 2:]
    j = pl.program_id(1)
    half = DIFF_ROWS // DIFF_KV_HEADS * n_tok

    def update(s_m, s_d, kc, dv):
        m_prev = mm_ref[...]
        m_new = jnp.maximum(m_prev, jnp.max(s_m, axis=-1, keepdims=True))
        alpha = jnp.exp(m_prev - m_new)
        p = jnp.exp(s_m - m_new)
        lm_ref[...] = alpha * lm_ref[...] + jnp.sum(p, axis=-1, keepdims=True)
        accm_ref[...] = alpha * accm_ref[...] + _dot(p.astype(BF16), kc)
        mm_ref[...] = m_new
        m_prev = md_ref[...]
        m_new = jnp.maximum(m_prev, jnp.max(s_d, axis=-1, keepdims=True))
        alpha = jnp.exp(m_prev - m_new)
        p = jnp.exp(s_d - m_new)
        ld_ref[...] = alpha * ld_ref[...] + jnp.sum(p, axis=-1, keepdims=True)
        p = p.astype(BF16)
        for g in range(DIFF_KV_HEADS):
            rows = slice(g * half, (g + 1) * half)
            accd_ref[rows, :] = alpha[rows] * accd_ref[rows, :] + _dot(p[rows], dv[g])
        md_ref[...] = m_new

    @pl.when(j == 0)
    def _():
        for h in range(MLA_HEADS):
            qm_ref[h * n_tok:(h + 1) * n_tok, :] = qabs_ref[:, h * MLA_KV_RANK:(h + 1) * MLA_KV_RANK]
            qp_ref[h * n_tok:(h + 1) * n_tok, :] = qpe_ref[:, h * MLA_ROPE:(h + 1) * MLA_ROPE]
        _fill_diff_queries(qd_ref, dq_ref, n_tok)
        mm_ref[...] = jnp.full(mm_ref.shape, NEG_INF, F32)
        lm_ref[...] = jnp.zeros(lm_ref.shape, F32)
        accm_ref[...] = jnp.zeros(accm_ref.shape, F32)
        md_ref[...] = jnp.full(md_ref.shape, NEG_INF, F32)
        ld_ref[...] = jnp.zeros(ld_ref.shape, F32)
        accd_ref[...] = jnp.zeros(accd_ref.shape, F32)
        for dst, src in ((new_c_ref, c_new_ref), (new_kr_ref, kr_new_ref),
                         (new_dk_ref, dk_new_ref), (new_dv_ref, dv_new_ref)):
            dst[...] = jnp.zeros(dst.shape, F32)
            dst[0:n_tok, :] = src[...]
        new_c = new_c_ref[...].astype(BF16)
        new_dv = new_dv_ref[...].astype(BF16)
        s_m = (_dot_nt(qm_ref[...].astype(BF16), new_c)
               + _dot_nt(qp_ref[...].astype(BF16), new_kr_ref[...].astype(BF16))) * MLA_SCALE
        s_d = _dot_nt(qd_ref[...].astype(BF16), new_dk_ref[...].astype(BF16))
        update(_causal(s_m, n_tok, 0, 0), _causal(s_d, n_tok, 0, 0), new_c,
               [new_dv[:, g * LANES:(g + 1) * LANES] for g in range(DIFF_KV_HEADS)])

    for p_ in range(pages):
        span = slice(p_ * PAGE_SIZE, (p_ + 1) * PAGE_SIZE)
        kc_ref[span, :] = page_refs[4 * p_][...].astype(BF16)
        krt_ref[:, span] = page_refs[4 * p_ + 1][...].astype(BF16)
        dkt_ref[:, span] = page_refs[4 * p_ + 2][...].astype(BF16)
        for g in range(DIFF_KV_HEADS):
            dv_ref[g, span, :] = page_refs[4 * p_ + 3][
                pl.ds(g, PAGE_SIZE, stride=DIFF_KV_HEADS), :].astype(BF16)
    kc = kc_ref[...]
    s_m = (_dot_nt(qm_ref[...].astype(BF16), kc)
           + _dot(qp_ref[...].astype(BF16), krt_ref[...])) * MLA_SCALE
    s_d = _dot(qd_ref[...].astype(BF16), dkt_ref[...])
    update(s_m, s_d, kc, [dv_ref[g] for g in range(DIFF_KV_HEADS)])

    @pl.when(j == pl.num_programs(1) - 1)
    def _():
        for h in range(MLA_HEADS):
            rows = slice(h * n_tok, (h + 1) * n_tok)
            mla_o_ref[:, h * MLA_KV_RANK:(h + 1) * MLA_KV_RANK] = accm_ref[rows, :] / lm_ref[rows, :]
        lam = _lambda(lq1_ref, lk1_ref, lq2_ref, lk2_ref, lam_init)
        _finish_diff(diff_o_ref, accd_ref, ld_ref, lam, g_sub_ref[...], lam_init, n_tok)


def _paged_attn(page_table, qabs, qpe, dq, c_new, kr_new, dk_new, dv_new, lams, g_sub,
                pools, n_tok, pages, lam_init):
    n_seq, n_pages = page_table.shape
    seq_row = lambda b, j, pt: (b, 0)
    const = lambda b, j, pt: (0, 0)
    page_specs = []
    page_args = []
    for p_ in range(pages):
        for pool in pools:
            page_specs.append(pl.BlockSpec(
                (None, None) + pool.shape[2:],
                functools.partial(lambda b, j, pt, p_: (0, pt[b, j * pages + p_], 0, 0), p_=p_)))
            page_args.append(pool)
    n_keys = pages * PAGE_SIZE
    mla_rows = MLA_HEADS * n_tok
    diff_rows = DIFF_ROWS * n_tok
    grid_spec = pltpu.PrefetchScalarGridSpec(
        num_scalar_prefetch=1,
        grid=(n_seq, n_pages // pages),
        in_specs=[pl.BlockSpec((n_tok, MLA_HEADS * MLA_KV_RANK), seq_row),
                  pl.BlockSpec((n_tok, MLA_HEADS * MLA_ROPE), seq_row),
                  pl.BlockSpec((n_tok, DIFF_WIDTH), seq_row),
                  pl.BlockSpec((n_tok, MLA_KV_RANK), seq_row),
                  pl.BlockSpec((n_tok, MLA_ROPE), seq_row),
                  pl.BlockSpec((n_tok, DIFF_KV_WIDTH), seq_row),
                  pl.BlockSpec((n_tok, DIFF_KV_WIDTH), seq_row)]
                 + [pl.BlockSpec((1, DIFF_HEAD_DIM), const)] * 4
                 + [pl.BlockSpec((1, 2 * DIFF_HEAD_DIM), const)]
                 + page_specs,
        out_specs=[pl.BlockSpec((n_tok, MLA_HEADS * MLA_KV_RANK), seq_row),
                   pl.BlockSpec((n_tok, DIFF_WIDTH), seq_row)],
        scratch_shapes=[pltpu.VMEM((mla_rows, MLA_KV_RANK), F32),
                        pltpu.VMEM((mla_rows, MLA_ROPE), F32),
                        pltpu.VMEM((diff_rows, DIFF_KV_WIDTH), F32),
                        pltpu.VMEM((n_keys, MLA_KV_RANK), BF16),
                        pltpu.VMEM((MLA_ROPE, n_keys), BF16),
                        pltpu.VMEM((DIFF_KV_WIDTH, n_keys), BF16),
                        pltpu.VMEM((DIFF_KV_HEADS, n_keys, 2 * DIFF_HEAD_DIM), BF16),
                        pltpu.VMEM((PAGE_SIZE, MLA_KV_RANK), F32),
                        pltpu.VMEM((PAGE_SIZE, MLA_ROPE), F32),
                        pltpu.VMEM((PAGE_SIZE, DIFF_KV_WIDTH), F32),
                        pltpu.VMEM((PAGE_SIZE, DIFF_KV_WIDTH), F32),
                        pltpu.VMEM((mla_rows, 1), F32),
                        pltpu.VMEM((mla_rows, 1), F32),
                        pltpu.VMEM((mla_rows, MLA_KV_RANK), F32),
                        pltpu.VMEM((diff_rows, 1), F32),
                        pltpu.VMEM((diff_rows, 1), F32),
                        pltpu.VMEM((diff_rows, 2 * DIFF_HEAD_DIM), F32)],
    )
    rows = n_seq * n_tok
    return pl.pallas_call(
        functools.partial(_paged_attn_kernel, n_tok=n_tok, pages=pages, lam_init=lam_init),
        grid_spec=grid_spec,
        out_shape=[jax.ShapeDtypeStruct((rows, MLA_HEADS * MLA_KV_RANK), F32),
                   jax.ShapeDtypeStruct((rows, DIFF_WIDTH), F32)],
        compiler_params=pltpu.CompilerParams(
            dimension_semantics=("arbitrary", "arbitrary"), vmem_limit_bytes=VMEM_LIMIT),
        name="paged_attn",
    )(page_table, qabs, qpe, dq, c_new, kr_new, dk_new, dv_new, *lams, g_sub, *page_args)


def _out_proj_kernel(x_ref, lat_ref, diff_ref, w_uv_ref, g_mla_ref, w_out_ref, h_ref, mla_ref):
    for h in range(MLA_HEADS):
        lat = lat_ref[:, h * MLA_KV_RANK:(h + 1) * MLA_KV_RANK].astype(BF16)
        mla_ref[:, h * MLA_V:(h + 1) * MLA_V] = _dot(lat, w_uv_ref[h])
    mla = _rms(mla_ref[...], g_mla_ref[...], NORM_EPS).astype(BF16)
    h_ref[...] = (x_ref[...] + _dot(mla, w_out_ref[0:MLA_WIDTH, :])
                  + _dot(diff_ref[...].astype(BF16), w_out_ref[MLA_WIDTH:MLA_WIDTH + DIFF_WIDTH, :]))


def _out_proj(x, lat, diff, w, tm):
    rows, d = x.shape
    row = lambda i: (i, 0)
    return pl.pallas_call(
        _out_proj_kernel,
        grid=(rows // tm,),
        in_specs=[pl.BlockSpec((tm, d), row),
                  pl.BlockSpec((tm, lat.shape[1]), row),
                  pl.BlockSpec((tm, diff.shape[1]), row),
                  _const_spec(w["w_uv"].shape),
                  _const_spec((1, MLA_WIDTH)),
                  _const_spec(w["w_out"].shape)],
        out_specs=pl.BlockSpec((tm, d), row),
        out_shape=jax.ShapeDtypeStruct((rows, d), F32),
        scratch_shapes=[pltpu.VMEM((tm, MLA_WIDTH), F32)],
        compiler_params=pltpu.CompilerParams(
            dimension_semantics=("arbitrary",), vmem_limit_bytes=VMEM_LIMIT),
        name="out_proj",
    )(x, lat, diff, w["w_uv"], w["g_mla_out"], w["w_out"])


def _ffn_kernel(h_ref, g_ffn_ref, w_gate_ref, w_up_ref, w_down_ref, g_final_ref, o_ref, xn_ref):
    j = pl.program_id(1)

    @pl.when(j == 0)
    def _():
        h = h_ref[...]
        xn_ref[...] = _rms(h, g_ffn_ref[...], NORM_EPS).astype(BF16)
        o_ref[...] = h

    xn = xn_ref[...]
    gate = _dot(xn, w_gate_ref[...])
    up = _dot(xn, w_up_ref[...])
    act = (gate * (1.0 / (1.0 + jnp.exp(-gate))) * up).astype(BF16)
    o_ref[...] += _dot(act, w_down_ref[...])

    @pl.when(j == pl.num_programs(1) - 1)
    def _():
        o_ref[...] = _rms(o_ref[...], g_final_ref[...], NORM_EPS)


def _ffn(h, w, tm, th):
    rows, d = h.shape
    hidden = w["w_gate"].shape[1]
    row = lambda i, j: (i, 0)
    return pl.pallas_call(
        _ffn_kernel,
        grid=(rows // tm, hidden // th),
        in_specs=[pl.BlockSpec((tm, d), row),
                  pl.BlockSpec((1, d), lambda i, j: (0, 0)),
                  pl.BlockSpec((d, th), lambda i, j: (0, j)),
                  pl.BlockSpec((d, th), lambda i, j: (0, j)),
                  pl.BlockSpec((th, d), lambda i, j: (j, 0)),
                  pl.BlockSpec((1, d), lambda i, j: (0, 0))],
        out_specs=pl.BlockSpec((tm, d), row),
        out_shape=jax.ShapeDtypeStruct((rows, d), F32),
        scratch_shapes=[pltpu.VMEM((tm, d), BF16)],
        compiler_params=pltpu.CompilerParams(
            dimension_semantics=("arbitrary", "arbitrary"), vmem_limit_bytes=VMEM_LIMIT),
        name="ffn",
    )(h, w["g_ffn"], w["w_gate"], w["w_up"], w["w_down"], w["g_final"])


def _rope_tables(pos, rot_dim):
    half = rot_dim // 2
    inv = ROPE_THETA ** (-jnp.arange(half, dtype=F32) * 2.0 / rot_dim)
    ang = pos.astype(F32)[:, None] * inv
    d = jnp.arange(LANES) % DIFF_HEAD_DIM
    a = ang[:, d % half]
    cos = jnp.where(d < rot_dim, jnp.cos(a), 1.0)
    sin = jnp.sin(a)
    sin_lo = jnp.where(d < half, -sin, 0.0)
    sin_hi = jnp.where((d >= half) & (d < rot_dim), sin, 0.0)
    return cos.astype(F32), sin_lo.astype(F32), sin_hi.astype(F32)


def _tables(pos):
    return _rope_tables(pos, MLA_ROPE) + _rope_tables(pos, DIFF_ROT)


def _row_block(rows, target):
    tm = min(rows, target)
    while rows % tm:
        tm //= 2
    return tm


def kernel(x_prompt, x_sample, cache_mla_latent, cache_mla_krope, cache_diff_k, cache_diff_v,
           page_table, meta_tokens, g_attn, w_in, g_q_a, w_q_b, g_kv_a, w_uk, w_uv, g_mla_out,
           lambda_q1, lambda_k1, lambda_q2, lambda_k2, g_subln, w_out, g_ffn, w_gate, w_up,
           w_down, g_final):
    batch, seq, d = x_prompt.shape
    n_seq, n_tok, _ = x_sample.shape
    depth = w_in.shape[0]
    assert depth == 1, "single-layer trunk"
    n_phys = cache_mla_latent.shape[1]
    past_len = page_table.shape[1] * PAGE_SIZE
    lam_init = 0.8 - 0.6 * math.exp(-0.3 * 0)

    wi = w_in[0]
    o = [0, MLA_Q_RANK, MLA_Q_RANK + MLA_KV_RANK, MLA_Q_RANK + MLA_KV_RANK + MLA_ROPE]
    o.append(o[3] + DIFF_WIDTH)
    o.append(o[4] + DIFF_KV_WIDTH)
    o.append(o[5] + DIFF_KV_WIDTH)
    w_in_r = jnp.concatenate(
        [wi[:, o[0]:o[2]], wi[:, o[3]:o[6]], wi[:, o[2]:o[3]],
         jnp.zeros((d, LANES - MLA_ROPE), wi.dtype)], axis=1).astype(BF16)
    wq = w_q_b[0]
    w_qb = jnp.concatenate(
        [wq[:, :, :MLA_QK_NOPE].reshape(MLA_Q_RANK, -1),
         wq[:, :, MLA_QK_NOPE:].reshape(MLA_Q_RANK, -1)], axis=1).astype(BF16)
    w = {
        "g_attn": g_attn[0][None], "w_in": w_in_r, "g_q_a": g_q_a[0][None],
        "g_kv_a": g_kv_a[0][None], "w_qb": w_qb,
        "w_uk": jnp.transpose(w_uk[0], (1, 2, 0)).astype(BF16),
        "w_uv": jnp.transpose(w_uv[0], (1, 0, 2)).astype(BF16),
        "g_mla_out": g_mla_out[0][None], "w_out": w_out[0].astype(BF16),
        "g_ffn": g_ffn[0][None], "w_gate": w_gate[0].astype(BF16),
        "w_up": w_up[0].astype(BF16), "w_down": w_down[0].astype(BF16),
        "g_final": g_final[None],
    }
    lams = (lambda_q1, lambda_k1, lambda_q2, lambda_k2)
    g_sub = g_subln[0][None]

    xp = x_prompt.reshape(batch * seq, d)
    xs = x_sample.reshape(n_seq * n_tok, d)
    tm_p = _row_block(seq, 256)
    tm_s = _row_block(n_seq * n_tok, 256)
    tab_p = _tables(N_META + jnp.arange(seq))
    tab_s = _tables(jnp.tile(past_len + jnp.arange(n_tok), tm_s // n_tok))
    tab_m = _tables(jnp.arange(N_META))
    (qabs_p, qpe_p, dq_p, c_p, kr_p, dk_p, dv_p, cb_p, krb_p, dkb_p, dvb_p) = _proj(
        xp, tab_p, tm_p, BF16, w)
    (qabs_s, qpe_s, dq_s, c_s, kr_s, dk_s, dv_s, _, _, _, _) = _proj(xs, tab_s, tm_s, F32, w)
    (_, _, _, c_m, kr_m, dk_m, dv_m, cb_m, krb_m, dkb_m, dvb_m) = _proj(
        meta_tokens.astype(F32), tab_m, N_META, BF16, w)

    lat_p = _mla_attn(qabs_p, qpe_p, cb_p, krb_p, cb_m, krb_m, batch, seq, _row_block(seq, 256))
    diff_p = _diff_attn(dq_p, dkb_p, dvb_p, dkb_m, dvb_m, lams, g_sub, batch, seq,
                        _row_block(seq, 128), _row_block(seq, 256), lam_init)

    pools = (cache_mla_latent,
             jnp.swapaxes(cache_mla_krope, 2, 3),
             jnp.transpose(cache_diff_k, (0, 1, 3, 4, 5, 2)).reshape(
                 depth, n_phys, DIFF_KV_WIDTH, PAGE_SIZE),
             cache_diff_v.reshape(depth, n_phys, PAGE_SIZE * DIFF_KV_HEADS, 2 * DIFF_HEAD_DIM))
    pages = _row_block(page_table.shape[1], 8)
    lat_s, diff_s = _paged_attn(page_table, qabs_s, qpe_s, dq_s, c_s, kr_s, dk_s, dv_s, lams,
                                g_sub, pools, n_tok, pages, lam_init)

    h_p = _out_proj(xp, lat_p, diff_p, w, _row_block(batch * seq, 512))
    h_s = _out_proj(xs, lat_s, diff_s, w, _row_block(n_seq * n_tok, 512))
    hidden = w_gate.shape[2]
    th = 512 if hidden % 512 == 0 else hidden
    y_p = _ffn(h_p, w, _row_block(batch * seq, 512), th).reshape(batch, seq, d)
    y_s = _ffn(h_s, w, _row_block(n_seq * n_tok, 512), th).reshape(n_seq, n_tok, d)

    def with_meta(meta, real):
        n = real.shape[-1]
        full = jnp.concatenate(
            [jnp.broadcast_to(meta[None], (batch, N_META, n)), real.reshape(batch, seq, n)], axis=1)
        return full[None]

    t = seq + N_META
    kv_shape = (DIFF_KV_HEADS, 2, DIFF_HEAD_DIM)
    v_shape = (DIFF_KV_HEADS, 2 * DIFF_HEAD_DIM)
    return (y_p, y_s,
            with_meta(c_m, c_p), with_meta(kr_m, kr_p),
            with_meta(dk_m, dk_p).reshape((1, batch, t) + kv_shape),
            with_meta(dv_m, dv_p).reshape((1, batch, t) + v_shape),
            c_s.reshape(1, n_seq, n_tok, -1), kr_s.reshape(1, n_seq, n_tok, -1),
            dk_s.reshape((1, n_seq, n_tok) + kv_shape),
            dv_s.reshape((1, n_seq, n_tok) + v_shape))
```

```python
import functools
import math

import jax
import jax.numpy as jnp
from jax import lax
from jax.experimental import pallas as pl
from jax.experimental.pallas import tpu as pltpu

F32 = jnp.float32
BF16 = jnp.bfloat16

N_META = 16
ROPE_THETA = 500000.0
NORM_EPS = 1e-6
SUBLN_EPS = 1e-5
NEG_INF = -1e30
PAGE_SIZE = 128

MLA_HEADS = 8
MLA_QK_NOPE = 128
MLA_ROPE = 64
MLA_V = 128
MLA_Q_RANK = 512
MLA_KV_RANK = 256
MLA_WIDTH = MLA_HEADS * MLA_V
MLA_SCALE = (MLA_QK_NOPE + MLA_ROPE) ** -0.5

DIFF_HEADS = 8
DIFF_KV_HEADS = 2
DIFF_REP = DIFF_HEADS // DIFF_KV_HEADS
DIFF_HEAD_DIM = 64
DIFF_ROT = DIFF_HEAD_DIM // 4
DIFF_WIDTH = DIFF_HEADS * 2 * DIFF_HEAD_DIM
DIFF_KV_WIDTH = DIFF_KV_HEADS * 2 * DIFF_HEAD_DIM
DIFF_SCALE = DIFF_HEAD_DIM ** -0.5
DIFF_ROWS = DIFF_KV_HEADS * DIFF_REP * 2

LANES = 128
VMEM_LIMIT = 56 * 1024 * 1024

OFF_QA = 0
OFF_KVA = OFF_QA + MLA_Q_RANK
OFF_DQ = OFF_KVA + MLA_KV_RANK
OFF_DK = OFF_DQ + DIFF_WIDTH
OFF_DV = OFF_DK + DIFF_KV_WIDTH
OFF_KR = OFF_DV + DIFF_KV_WIDTH
IN_WIDTH_PADDED = OFF_KR + LANES


def _rms(x, g, eps):
    return x * lax.rsqrt(jnp.mean(x * x, axis=-1, keepdims=True) + eps) * g


def _dot(a, b):
    return jnp.dot(a, b, preferred_element_type=F32)


def _dot_nt(a, b):
    return lax.dot_general(a, b, (((1,), (1,)), ((), ())), preferred_element_type=F32)


def _rope_chunk(x, cos, sin_lo, sin_hi, half):
    return (x * cos + pltpu.roll(x, LANES - half, 1) * sin_lo
            + pltpu.roll(x, half, 1) * sin_hi)


def _causal(s, n_tok, tok0, col0):
    assert n_tok & (n_tok - 1) == 0
    tok = tok0 + (lax.broadcasted_iota(jnp.int32, s.shape, 0) & (n_tok - 1))
    col = col0 + lax.broadcasted_iota(jnp.int32, s.shape, 1)
    return jnp.where(col <= tok, s, NEG_INF)


def _const_spec(shape):
    return pl.BlockSpec(shape, lambda *_: (0,) * len(shape))


def _proj_kernel(x_ref, g_attn_ref, w_in_ref, g_q_ref, g_kv_ref, w_qb_ref, w_uk_ref,
                 cos_m_ref, slo_m_ref, shi_m_ref, cos_d_ref, slo_d_ref, shi_d_ref,
                 qabs_ref, qpe_ref, dq_ref, c_ref, kr_ref, dk_ref, dv_ref,
                 cb_ref, krb_ref, dkb_ref, dvb_ref):
    xn = _rms(x_ref[...], g_attn_ref[...], NORM_EPS).astype(BF16)
    z = _dot(xn, w_in_ref[...])

    cos_m, slo_m, shi_m = cos_m_ref[...], slo_m_ref[...], shi_m_ref[...]
    cos_d, slo_d, shi_d = cos_d_ref[...], slo_d_ref[...], shi_d_ref[...]

    c = _rms(z[:, OFF_KVA:OFF_KVA + MLA_KV_RANK], g_kv_ref[...], NORM_EPS)
    c_ref[...] = c
    cb_ref[...] = c.astype(BF16)
    kr = _rope_chunk(z[:, OFF_KR:OFF_KR + LANES], cos_m, slo_m, shi_m, MLA_ROPE // 2)
    kr_ref[...] = kr[:, :MLA_ROPE]
    krb_ref[...] = kr[:, :MLA_ROPE].astype(BF16)

    qn = _rms(z[:, OFF_QA:OFF_QA + MLA_Q_RANK], g_q_ref[...], NORM_EPS).astype(BF16)
    q = _dot(qn, w_qb_ref[...])
    nope_w = MLA_HEADS * MLA_QK_NOPE
    for h in range(MLA_HEADS):
        qh = q[:, h * MLA_QK_NOPE:(h + 1) * MLA_QK_NOPE].astype(BF16)
        qabs_ref[:, h * MLA_KV_RANK:(h + 1) * MLA_KV_RANK] = _dot(qh, w_uk_ref[h]).astype(qabs_ref.dtype)
    for j in range(MLA_HEADS * MLA_ROPE // LANES):
        qc = q[:, nope_w + j * LANES:nope_w + (j + 1) * LANES]
        qpe_ref[:, j * LANES:(j + 1) * LANES] = _rope_chunk(
            qc, cos_m, slo_m, shi_m, MLA_ROPE // 2).astype(qpe_ref.dtype)

    for j in range(DIFF_WIDTH // LANES):
        xc = z[:, OFF_DQ + j * LANES:OFF_DQ + (j + 1) * LANES]
        dq_ref[:, j * LANES:(j + 1) * LANES] = _rope_chunk(
            xc, cos_d, slo_d, shi_d, DIFF_ROT // 2).astype(dq_ref.dtype)
    for j in range(DIFF_KV_WIDTH // LANES):
        xc = z[:, OFF_DK + j * LANES:OFF_DK + (j + 1) * LANES]
        dk = _rope_chunk(xc, cos_d, slo_d, shi_d, DIFF_ROT // 2)
        dk_ref[:, j * LANES:(j + 1) * LANES] = dk
        dkb_ref[:, j * LANES:(j + 1) * LANES] = dk.astype(BF16)
    dv = z[:, OFF_DV:OFF_DV + DIFF_KV_WIDTH]
    dv_ref[...] = dv
    dvb_ref[...] = dv.astype(BF16)


def _proj(x, tables, tm, q_dtype, w):
    rows, d = x.shape
    t_tab = tables[0].shape[0]
    n_tab = t_tab // tm
    row = lambda i: (i, 0)
    tab = lambda i: (i % n_tab, 0)
    out_widths = [(MLA_HEADS * MLA_KV_RANK, q_dtype), (MLA_HEADS * MLA_ROPE, q_dtype),
                  (DIFF_WIDTH, q_dtype), (MLA_KV_RANK, F32), (MLA_ROPE, F32),
                  (DIFF_KV_WIDTH, F32), (DIFF_KV_WIDTH, F32), (MLA_KV_RANK, BF16),
                  (MLA_ROPE, BF16), (DIFF_KV_WIDTH, BF16), (DIFF_KV_WIDTH, BF16)]
    return pl.pallas_call(
        _proj_kernel,
        grid=(rows // tm,),
        in_specs=[pl.BlockSpec((tm, d), row),
                  _const_spec((1, d)),
                  _const_spec(w["w_in"].shape),
                  _const_spec((1, MLA_Q_RANK)),
                  _const_spec((1, MLA_KV_RANK)),
                  _const_spec(w["w_qb"].shape),
                  _const_spec(w["w_uk"].shape)]
                 + [pl.BlockSpec((tm, LANES), tab)] * 6,
        out_specs=[pl.BlockSpec((tm, n), row) for n, _ in out_widths],
        out_shape=[jax.ShapeDtypeStruct((rows, n), dt) for n, dt in out_widths],
        compiler_params=pltpu.CompilerParams(
            dimension_semantics=("arbitrary",), vmem_limit_bytes=VMEM_LIMIT),
        name="proj",
    )(x, w["g_attn"], w["w_in"], w["g_q_a"], w["g_kv_a"], w["w_qb"], w["w_uk"], *tables)


def _t(x):
    return x.astype(F32).T


def _online_softmax_t(s, m_prev, l_prev, log2_scale):
    m_new = jnp.maximum(m_prev, jnp.max(s, axis=0, keepdims=True))
    alpha = jnp.exp2((m_prev - m_new) * log2_scale)
    p = jnp.exp2((s - m_new) * log2_scale)
    return p, alpha, m_new, alpha * l_prev + jnp.sum(p, axis=0, keepdims=True)


def _mla_attn_kernel(qabs_ref, qpe_ref, cb_ref, krb_ref, cmeta_ref, krmeta_ref, o_ref,
                     qt_ref, ct_ref, cm_ref, krm_ref, cmt_ref, m_ref, l_ref, acc_ref, s_ref, p_ref,
                     a_ref, *, bq):
    qi = pl.program_id(1)
    seq = cb_ref.shape[0]
    log2_scale = MLA_SCALE * math.log2(math.e)

    @pl.when(qi == 0)
    def _():
        for kb in range(seq // bq):
            ct_ref[kb] = _t(cb_ref[kb * bq:(kb + 1) * bq, :]).astype(BF16)
        cm_ref[...] = jnp.zeros(cm_ref.shape, BF16)
        cm_ref[0:N_META, :] = cmeta_ref[...]
        krm_ref[...] = jnp.zeros(krm_ref.shape, BF16)
        krm_ref[0:N_META, :] = krmeta_ref[...]
        cmt_ref[...] = _t(cm_ref[...]).astype(BF16)

    for h in range(MLA_HEADS):
        qt_ref[h, 0:MLA_KV_RANK, :] = _t(
            qabs_ref[:, h * MLA_KV_RANK:(h + 1) * MLA_KV_RANK]).astype(BF16)
    per_chunk = LANES // MLA_ROPE
    for j in range(MLA_HEADS // per_chunk):
        t = _t(qpe_ref[:, j * LANES:(j + 1) * LANES]).astype(BF16)
        for i in range(per_chunk):
            qt_ref[j * per_chunk + i, MLA_KV_RANK:MLA_KV_RANK + MLA_ROPE, :] = (
                t[i * MLA_ROPE:(i + 1) * MLA_ROPE])
    m_ref[...] = jnp.full(m_ref.shape, NEG_INF, F32)
    l_ref[...] = jnp.zeros(l_ref.shape, F32)
    acc_ref[...] = jnp.zeros(acc_ref.shape, F32)

    def attend(kc, kkr, ct, visible):
        n = kc.shape[0]
        for h in range(MLA_HEADS):
            s_ref[h, 0:n, :] = (_dot(kc, qt_ref[h, 0:MLA_KV_RANK, :])
                                + _dot(kkr, qt_ref[h, MLA_KV_RANK:MLA_KV_RANK + MLA_ROPE, :]))
        for h in range(MLA_HEADS):
            s = s_ref[h, 0:n, :]
            if visible is not None:
                s = jnp.where(visible, s, NEG_INF)
            p, alpha, m_new, l_new = _online_softmax_t(
                s, m_ref[h:h + 1, :], l_ref[h:h + 1, :], log2_scale)
            p_ref[h, 0:n, :] = p.astype(BF16)
            a_ref[h:h + 1, :] = alpha
            m_ref[h:h + 1, :] = m_new
            l_ref[h:h + 1, :] = l_new
        for h in range(MLA_HEADS):
            acc_ref[h] = a_ref[h:h + 1, :] * acc_ref[h] + _dot(ct, p_ref[h, 0:n, :])

    key = lax.broadcasted_iota(jnp.int32, (LANES, bq), 0)
    attend(cm_ref[...], krm_ref[...], cmt_ref[...], key < N_META)

    def body(kb, carry):
        start = pl.multiple_of(kb * bq, bq)
        attend(cb_ref[pl.ds(start, bq), :], krb_ref[pl.ds(start, bq), :], ct_ref[kb], None)
        return carry

    lax.fori_loop(0, qi, body, 0)
    start = pl.multiple_of(qi * bq, bq)
    key = lax.broadcasted_iota(jnp.int32, (bq, bq), 0)
    tok = lax.broadcasted_iota(jnp.int32, (bq, bq), 1)
    attend(cb_ref[pl.ds(start, bq), :], krb_ref[pl.ds(start, bq), :], ct_ref[qi], key <= tok)

    for h in range(MLA_HEADS):
        o_ref[:, h * MLA_KV_RANK:(h + 1) * MLA_KV_RANK] = (
            acc_ref[h] / l_ref[h:h + 1, :]).T.astype(o_ref.dtype)


def _mla_attn(qabs, qpe, cb, krb, cmeta, krmeta, batch, seq, bq):
    nq = seq // bq
    qrow = lambda b, i: (b * nq + i, 0)
    kv = lambda b, i: (b, 0)
    return pl.pallas_call(
        functools.partial(_mla_attn_kernel, bq=bq),
        grid=(batch, nq),
        in_specs=[pl.BlockSpec((bq, MLA_HEADS * MLA_KV_RANK), qrow),
                  pl.BlockSpec((bq, MLA_HEADS * MLA_ROPE), qrow),
                  pl.BlockSpec((seq, MLA_KV_RANK), kv),
                  pl.BlockSpec((seq, MLA_ROPE), kv),
                  _const_spec(cmeta.shape),
                  _const_spec(krmeta.shape)],
        out_specs=pl.BlockSpec((bq, MLA_HEADS * MLA_KV_RANK), qrow),
        out_shape=jax.ShapeDtypeStruct((batch * seq, MLA_HEADS * MLA_KV_RANK), BF16),
        scratch_shapes=[pltpu.VMEM((MLA_HEADS, MLA_KV_RANK + MLA_ROPE, bq), BF16),
                        pltpu.VMEM((nq, MLA_KV_RANK, bq), BF16),
                        pltpu.VMEM((LANES, MLA_KV_RANK), BF16),
                        pltpu.VMEM((LANES, MLA_ROPE), BF16),
                        pltpu.VMEM((MLA_KV_RANK, LANES), BF16),
                        pltpu.VMEM((MLA_HEADS, bq), F32),
                        pltpu.VMEM((MLA_HEADS, bq), F32),
                        pltpu.VMEM((MLA_HEADS, MLA_KV_RANK, bq), F32),
                        pltpu.VMEM((MLA_HEADS, bq, bq), F32),
                        pltpu.VMEM((MLA_HEADS, bq, bq), BF16),
                        pltpu.VMEM((MLA_HEADS, bq), F32)],
        compiler_params=pltpu.CompilerParams(
            dimension_semantics=("arbitrary", "arbitrary"), vmem_limit_bytes=VMEM_LIMIT),
        name="mla_attn",
    )(qabs, qpe, cb, krb, cmeta, krmeta)


def _lambda(lq1_ref, lk1_ref, lq2_ref, lk2_ref, lam_init):
    a = jnp.sum(lq1_ref[...] * lk1_ref[...], axis=-1, keepdims=True)
    b = jnp.sum(lq2_ref[...] * lk2_ref[...], axis=-1, keepdims=True)
    return jnp.exp(a) - jnp.exp(b) + lam_init


def _fill_diff_queries(qd_ref, dq_ref, n_tok):
    qd_ref[...] = jnp.zeros(qd_ref.shape, qd_ref.dtype)
    lane = lax.broadcasted_iota(jnp.int32, (n_tok, LANES), 1)
    for g in range(DIFF_KV_HEADS):
        for r in range(DIFF_REP):
            col = (g * DIFF_REP + r) * LANES
            x = dq_ref[:, col:col + LANES] * DIFF_SCALE
            for m in range(2):
                row = ((g * DIFF_REP + r) * 2 + m) * n_tok
                keep = (lane < DIFF_HEAD_DIM) if m == 0 else (lane >= DIFF_HEAD_DIM)
                qd_ref[row:row + n_tok, g * LANES:(g + 1) * LANES] = jnp.where(
                    keep, x, jnp.zeros_like(x)).astype(qd_ref.dtype)


def _finish_diff(o_ref, acc_ref, l_ref, lam, g_sub, lam_init, n_tok):
    for g in range(DIFF_KV_HEADS):
        for r in range(DIFF_REP):
            r0 = ((g * DIFF_REP + r) * 2) * n_tok
            r1 = r0 + n_tok
            o0 = acc_ref[r0:r0 + n_tok, :] / l_ref[r0:r0 + n_tok, :]
            o1 = acc_ref[r1:r1 + n_tok, :] / l_ref[r1:r1 + n_tok, :]
            d = _rms(o0 - lam * o1, g_sub, SUBLN_EPS) * (1.0 - lam_init)
            col = (g * DIFF_REP + r) * LANES
            o_ref[:, col:col + LANES] = d.astype(o_ref.dtype)


def _diff_attn_kernel(dq_ref, dkb_ref, dvb_ref, dkmeta_ref, dvmeta_ref,
                      lq1_ref, lk1_ref, lq2_ref, lk2_ref, g_sub_ref, o_ref,
                      qt_ref, vt_ref, km_ref, vm_ref, vmt_ref, m_ref, l_ref, acc_ref,
                      s_ref, p_ref, a_ref, *, bq, bk, lam_init):
    qi = pl.program_id(1)
    seq = dkb_ref.shape[0]
    n_chunks = DIFF_KV_HEADS * DIFF_REP
    log2e = math.log2(math.e)

    @pl.when(qi == 0)
    def _():
        for kb in range(seq // bk):
            vt_ref[kb] = _t(dvb_ref[kb * bk:(kb + 1) * bk, :]).astype(BF16)
        km_ref[...] = jnp.zeros(km_ref.shape, BF16)
        km_ref[0:N_META, :] = dkmeta_ref[...]
        vm_ref[...] = jnp.zeros(vm_ref.shape, BF16)
        vm_ref[0:N_META, :] = dvmeta_ref[...]
        vmt_ref[...] = _t(vm_ref[...]).astype(BF16)

    dim = lax.broadcasted_iota(jnp.int32, (LANES, bq), 0)
    for c in range(n_chunks):
        x = _t(dq_ref[:, c * LANES:(c + 1) * LANES] * DIFF_SCALE)
        qt_ref[c, :, 0:bq] = jnp.where(dim < DIFF_HEAD_DIM, x, 0.0).astype(BF16)
        qt_ref[c, :, bq:2 * bq] = jnp.where(dim >= DIFF_HEAD_DIM, x, 0.0).astype(BF16)
    m_ref[...] = jnp.full(m_ref.shape, NEG_INF, F32)
    l_ref[...] = jnp.zeros(l_ref.shape, F32)
    acc_ref[...] = jnp.zeros(acc_ref.shape, F32)

    def attend(dk, vt, visible):
        n = dk.shape[0]
        for c in range(n_chunks):
            g = c // DIFF_REP
            s_ref[c, 0:n, :] = _dot(dk[:, g * LANES:(g + 1) * LANES], qt_ref[c])
        for c in range(n_chunks):
            s = s_ref[c, 0:n, :]
            if visible is not None:
                s = jnp.where(visible, s, NEG_INF)
            p, alpha, m_new, l_new = _online_softmax_t(
                s, m_ref[c:c + 1, :], l_ref[c:c + 1, :], log2e)
            p_ref[c, 0:n, :] = p.astype(BF16)
            a_ref[c:c + 1, :] = alpha
            m_ref[c:c + 1, :] = m_new
            l_ref[c:c + 1, :] = l_new
        for c in range(n_chunks):
            g = c // DIFF_REP
            acc_ref[c] = a_ref[c:c + 1, :] * acc_ref[c] + _dot(
                vt[g * LANES:(g + 1) * LANES, :], p_ref[c, 0:n, :])

    key = lax.broadcasted_iota(jnp.int32, (LANES, 2 * bq), 0)
    attend(km_ref[...], vmt_ref[...], key < N_META)

    n_full = (qi * bq) // bk

    def body(kb, carry):
        start = pl.multiple_of(kb * bk, bk)
        attend(dkb_ref[pl.ds(start, bk), :], vt_ref[kb], None)
        return carry

    lax.fori_loop(0, n_full, body, 0)
    start = pl.multiple_of(n_full * bk, bk)
    key = start + lax.broadcasted_iota(jnp.int32, (bk, 2 * bq), 0)
    tok = qi * bq + (lax.broadcasted_iota(jnp.int32, (bk, 2 * bq), 1) & (bq - 1))
    attend(dkb_ref[pl.ds(start, bk), :], vt_ref[n_full], key <= tok)

    lam = _lambda(lq1_ref, lk1_ref, lq2_ref, lk2_ref, lam_init)
    for c in range(n_chunks):
        o = acc_ref[c] / l_ref[c:c + 1, :]
        d = (o[:, 0:bq] - lam * o[:, bq:2 * bq]).T
        d = _rms(d, g_sub_ref[...], SUBLN_EPS) * (1.0 - lam_init)
        o_ref[:, c * LANES:(c + 1) * LANES] = d.astype(o_ref.dtype)


def _diff_attn(dq, dkb, dvb, dkmeta, dvmeta, lams, g_sub, batch, seq, bq, bk, lam_init):
    assert bq & (bq - 1) == 0 and bk % bq == 0 and seq % bk == 0
    nq = seq // bq
    n_chunks = DIFF_KV_HEADS * DIFF_REP
    qrow = lambda b, i: (b * nq + i, 0)
    kv = lambda b, i: (b, 0)
    return pl.pallas_call(
        functools.partial(_diff_attn_kernel, bq=bq, bk=bk, lam_init=lam_init),
        grid=(batch, nq),
        in_specs=[pl.BlockSpec((bq, DIFF_WIDTH), qrow),
                  pl.BlockSpec((seq, DIFF_KV_WIDTH), kv),
                  pl.BlockSpec((seq, DIFF_KV_WIDTH), kv),
                  _const_spec(dkmeta.shape),
                  _const_spec(dvmeta.shape)]
                 + [_const_spec((1, DIFF_HEAD_DIM))] * 4
                 + [_const_spec((1, 2 * DIFF_HEAD_DIM))],
        out_specs=pl.BlockSpec((bq, DIFF_WIDTH), qrow),
        out_shape=jax.ShapeDtypeStruct((batch * seq, DIFF_WIDTH), BF16),
        scratch_shapes=[pltpu.VMEM((n_chunks, LANES, 2 * bq), BF16),
                        pltpu.VMEM((seq // bk, DIFF_KV_WIDTH, bk), BF16),
                        pltpu.VMEM((LANES, DIFF_KV_WIDTH), BF16),
                        pltpu.VMEM((LANES, DIFF_KV_WIDTH), BF16),
                        pltpu.VMEM((DIFF_KV_WIDTH, LANES), BF16),
                        pltpu.VMEM((n_chunks, 2 * bq), F32),
                        pltpu.VMEM((n_chunks, 2 * bq), F32),
                        pltpu.VMEM((n_chunks, 2 * DIFF_HEAD_DIM, 2 * bq), F32),
                        pltpu.VMEM((n_chunks, bk, 2 * bq), F32),
                        pltpu.VMEM((n_chunks, bk, 2 * bq), BF16),
                        pltpu.VMEM((n_chunks, 2 * bq), F32)],
        compiler_params=pltpu.CompilerParams(
            dimension_semantics=("arbitrary", "arbitrary"), vmem_limit_bytes=VMEM_LIMIT),
        name="diff_attn",
    )(dq, dkb, dvb, dkmeta, dvmeta, *lams, g_sub)


def _paged_attn_kernel(pt_ref, qabs_ref, qpe_ref, dq_ref, c_new_ref, kr_new_ref, dk_new_ref,
                       dv_new_ref, lq1_ref, lk1_ref, lq2_ref, lk2_ref, g_sub_ref, *rest,
                       n_tok, pages, lam_init):
    del pt_ref
    page_refs = rest[:4 * pages]
    mla_o_ref, diff_o_ref = rest[4 * pages:4 * pages + 2]
    (qm_ref, qp_ref, qd_ref, kc_ref, krt_ref, dkt_ref, dv_ref, new_c_ref, new_kr_ref,
     new_dk_ref, new_dv_ref, mm_ref, lm_ref, accm_ref, md_ref, ld_ref,
     accd_ref) = rest[4 * pages + 2:]
    j = pl.program_id(1)
    half = DIFF_ROWS // DIFF_KV_HEADS * n_tok

    def update(s_m, s_d, mix_m, mix_d):
        m_prev = mm_ref[...]
        m_new = jnp.maximum(m_prev, jnp.max(s_m, axis=-1, keepdims=True))
        alpha = jnp.exp(m_prev - m_new)
        p = jnp.exp(s_m - m_new)
        lm_ref[...] = alpha * lm_ref[...] + jnp.sum(p, axis=-1, keepdims=True)
        accm_ref[...] = alpha * accm_ref[...] + mix_m(p.astype(BF16))
        mm_ref[...] = m_new
        m_prev = md_ref[...]
        m_new = jnp.maximum(m_prev, jnp.max(s_d, axis=-1, keepdims=True))
        alpha = jnp.exp(m_prev - m_new)
        p = jnp.exp(s_d - m_new)
        ld_ref[...] = alpha * ld_ref[...] + jnp.sum(p, axis=-1, keepdims=True)
        p = p.astype(BF16)
        for g in range(DIFF_KV_HEADS):
            rows = slice(g * half, (g + 1) * half)
            accd_ref[rows, :] = alpha[rows] * accd_ref[rows, :] + mix_d(p[rows], g)
        md_ref[...] = m_new

    @pl.when(j == 0)
    def _():
        for h in range(MLA_HEADS):
            qm_ref[h * n_tok:(h + 1) * n_tok, :] = qabs_ref[:, h * MLA_KV_RANK:(h + 1) * MLA_KV_RANK]
            qp_ref[h * n_tok:(h + 1) * n_tok, :] = qpe_ref[:, h * MLA_ROPE:(h + 1) * MLA_ROPE]
        _fill_diff_queries(qd_ref, dq_ref, n_tok)
        mm_ref[...] = jnp.full(mm_ref.shape, NEG_INF, F32)
        lm_ref[...] = jnp.zeros(lm_ref.shape, F32)
        accm_ref[...] = jnp.zeros(accm_ref.shape, F32)
        md_ref[...] = jnp.full(md_ref.shape, NEG_INF, F32)
        ld_ref[...] = jnp.zeros(ld_ref.shape, F32)
        accd_ref[...] = jnp.zeros(accd_ref.shape, F32)
        for dst, src in ((new_c_ref, c_new_ref), (new_kr_ref, kr_new_ref),
                         (new_dk_ref, dk_new_ref), (new_dv_ref, dv_new_ref)):
            dst[...] = jnp.zeros(dst.shape, F32)
            dst[0:n_tok, :] = src[...]
        new_c = new_c_ref[...].astype(BF16)
        new_dv = new_dv_ref[...].astype(BF16)
        s_m = (_dot_nt(qm_ref[...].astype(BF16), new_c)
               + _dot_nt(qp_ref[...].astype(BF16), new_kr_ref[...].astype(BF16))) * MLA_SCALE
        s_d = _dot_nt(qd_ref[...].astype(BF16), new_dk_ref[...].astype(BF16))
        update(_causal(s_m, n_tok, 0, 0), _causal(s_d, n_tok, 0, 0),
               lambda p: _dot(p, new_c),
               lambda p, g: _dot(p, new_dv[:, g * LANES:(g + 1) * LANES]))

    for p_ in range(pages):
        span = slice(p_ * PAGE_SIZE, (p_ + 1) * PAGE_SIZE)
        kc_ref[span, :] = page_refs[4 * p_][...].astype(BF16)
        krt_ref[:, span] = page_refs[4 * p_ + 1][...].astype(BF16)
        dkt_ref[:, span] = page_refs[4 * p_ + 2][...].astype(BF16)
        for g in range(DIFF_KV_HEADS):
            dv_ref[g, span, :] = page_refs[4 * p_ + 3][
                pl.ds(g, PAGE_SIZE, stride=DIFF_KV_HEADS), :].astype(BF16)
    s_m = (_dot_nt(qm_ref[...].astype(BF16), kc_ref[...])
           + _dot(qp_ref[...].astype(BF16), krt_ref[...])) * MLA_SCALE
    s_d = _dot(qd_ref[...].astype(BF16), dkt_ref[...])
    update(s_m, s_d, lambda p: _dot(p, kc_ref[...]), lambda p, g: _dot(p, dv_ref[g]))

    @pl.when(j == pl.num_programs(1) - 1)
    def _():
        for h in range(MLA_HEADS):
            rows = slice(h * n_tok, (h + 1) * n_tok)
            mla_o_ref[:, h * MLA_KV_RANK:(h + 1) * MLA_KV_RANK] = accm_ref[rows, :] / lm_ref[rows, :]
        lam = _lambda(lq1_ref, lk1_ref, lq2_ref, lk2_ref, lam_init)
        _finish_diff(diff_o_ref, accd_ref, ld_ref, lam, g_sub_ref[...], lam_init, n_tok)


def _paged_attn(page_table, qabs, qpe, dq, c_new, kr_new, dk_new, dv_new, lams, g_sub,
                pools, n_tok, pages, lam_init):
    n_seq, n_pages = page_table.shape
    seq_row = lambda b, j, pt: (b, 0)
    const = lambda b, j, pt: (0, 0)
    page_specs = []
    page_args = []
    for p_ in range(pages):
        for pool in pools:
            page_specs.append(pl.BlockSpec(
                (None, None) + pool.shape[2:],
                functools.partial(lambda b, j, pt, p_: (0, pt[b, j * pages + p_], 0, 0), p_=p_)))
            page_args.append(pool)
    n_keys = pages * PAGE_SIZE
    mla_rows = MLA_HEADS * n_tok
    diff_rows = DIFF_ROWS * n_tok
    grid_spec = pltpu.PrefetchScalarGridSpec(
        num_scalar_prefetch=1,
        grid=(n_seq, n_pages // pages),
        in_specs=[pl.BlockSpec((n_tok, MLA_HEADS * MLA_KV_RANK), seq_row),
                  pl.BlockSpec((n_tok, MLA_HEADS * MLA_ROPE), seq_row),
                  pl.BlockSpec((n_tok, DIFF_WIDTH), seq_row),
                  pl.BlockSpec((n_tok, MLA_KV_RANK), seq_row),
                  pl.BlockSpec((n_tok, MLA_ROPE), seq_row),
                  pl.BlockSpec((n_tok, DIFF_KV_WIDTH), seq_row),
                  pl.BlockSpec((n_tok, DIFF_KV_WIDTH), seq_row)]
                 + [pl.BlockSpec((1, DIFF_HEAD_DIM), const)] * 4
                 + [pl.BlockSpec((1, 2 * DIFF_HEAD_DIM), const)]
                 + page_specs,
        out_specs=[pl.BlockSpec((n_tok, MLA_HEADS * MLA_KV_RANK), seq_row),
                   pl.BlockSpec((n_tok, DIFF_WIDTH), seq_row)],
        scratch_shapes=[pltpu.VMEM((mla_rows, MLA_KV_RANK), F32),
                        pltpu.VMEM((mla_rows, MLA_ROPE), F32),
                        pltpu.VMEM((diff_rows, DIFF_KV_WIDTH), F32),
                        pltpu.VMEM((n_keys, MLA_KV_RANK), BF16),
                        pltpu.VMEM((MLA_ROPE, n_keys), BF16),
                        pltpu.VMEM((DIFF_KV_WIDTH, n_keys), BF16),
                        pltpu.VMEM((DIFF_KV_HEADS, n_keys, 2 * DIFF_HEAD_DIM), BF16),
                        pltpu.VMEM((PAGE_SIZE, MLA_KV_RANK), F32),
                        pltpu.VMEM((PAGE_SIZE, MLA_ROPE), F32),
                        pltpu.VMEM((PAGE_SIZE, DIFF_KV_WIDTH), F32),
                        pltpu.VMEM((PAGE_SIZE, DIFF_KV_WIDTH), F32),
                        pltpu.VMEM((mla_rows, 1), F32),
                        pltpu.VMEM((mla_rows, 1), F32),
                        pltpu.VMEM((mla_rows, MLA_KV_RANK), F32),
                        pltpu.VMEM((diff_rows, 1), F32),
                        pltpu.VMEM((diff_rows, 1), F32),
                        pltpu.VMEM((diff_rows, 2 * DIFF_HEAD_DIM), F32)],
    )
    rows = n_seq * n_tok
    return pl.pallas_call(
        functools.partial(_paged_attn_kernel, n_tok=n_tok, pages=pages, lam_init=lam_init),
        grid_spec=grid_spec,
        out_shape=[jax.ShapeDtypeStruct((rows, MLA_HEADS * MLA_KV_RANK), F32),
                   jax.ShapeDtypeStruct((rows, DIFF_WIDTH), F32)],
        compiler_params=pltpu.CompilerParams(
            dimension_semantics=("arbitrary", "arbitrary"), vmem_limit_bytes=VMEM_LIMIT),
        name="paged_attn",
    )(page_table, qabs, qpe, dq, c_new, kr_new, dk_new, dv_new, *lams, g_sub, *page_args)


def _out_proj_kernel(x_ref, lat_ref, diff_ref, w_uv_ref, g_mla_ref, w_out_ref, h_ref, mla_ref):
    for h in range(MLA_HEADS):
        lat = lat_ref[:, h * MLA_KV_RANK:(h + 1) * MLA_KV_RANK].astype(BF16)
        mla_ref[:, h * MLA_V:(h + 1) * MLA_V] = _dot(lat, w_uv_ref[h])
    mla = _rms(mla_ref[...], g_mla_ref[...], NORM_EPS).astype(BF16)
    h_ref[...] = (x_ref[...] + _dot(mla, w_out_ref[0:MLA_WIDTH, :])
                  + _dot(diff_ref[...].astype(BF16), w_out_ref[MLA_WIDTH:MLA_WIDTH + DIFF_WIDTH, :]))


def _out_proj(x, lat, diff, w, tm):
    rows, d = x.shape
    row = lambda i: (i, 0)
    return pl.pallas_call(
        _out_proj_kernel,
        grid=(rows // tm,),
        in_specs=[pl.BlockSpec((tm, d), row),
                  pl.BlockSpec((tm, lat.shape[1]), row),
                  pl.BlockSpec((tm, diff.shape[1]), row),
                  _const_spec(w["w_uv"].shape),
                  _const_spec((1, MLA_WIDTH)),
                  _const_spec(w["w_out"].shape)],
        out_specs=pl.BlockSpec((tm, d), row),
        out_shape=jax.ShapeDtypeStruct((rows, d), F32),
        scratch_shapes=[pltpu.VMEM((tm, MLA_WIDTH), F32)],
        compiler_params=pltpu.CompilerParams(
            dimension_semantics=("arbitrary",), vmem_limit_bytes=VMEM_LIMIT),
        name="out_proj",
    )(x, lat, diff, w["w_uv"], w["g_mla_out"], w["w_out"])


def _ffn_kernel(h_ref, g_ffn_ref, w_gate_ref, w_up_ref, w_down_ref, g_final_ref, o_ref, xn_ref):
    j = pl.program_id(1)

    @pl.when(j == 0)
    def _():
        h = h_ref[...]
        xn_ref[...] = _rms(h, g_ffn_ref[...], NORM_EPS).astype(BF16)
        o_ref[...] = h

    xn = xn_ref[...]
    gate = _dot(xn, w_gate_ref[...])
    up = _dot(xn, w_up_ref[...])
    act = (gate * (1.0 / (1.0 + jnp.exp(-gate))) * up).astype(BF16)
    o_ref[...] += _dot(act, w_down_ref[...])

    @pl.when(j == pl.num_programs(1) - 1)
    def _():
        o_ref[...] = _rms(o_ref[...], g_final_ref[...], NORM_EPS)


def _ffn(h, w, tm, th):
    rows, d = h.shape
    hidden = w["w_gate"].shape[1]
    row = lambda i, j: (i, 0)
    return pl.pallas_call(
        _ffn_kernel,
        grid=(rows // tm, hidden // th),
        in_specs=[pl.BlockSpec((tm, d), row),
                  pl.BlockSpec((1, d), lambda i, j: (0, 0)),
                  pl.BlockSpec((d, th), lambda i, j: (0, j)),
                  pl.BlockSpec((d, th), lambda i, j: (0, j)),
                  pl.BlockSpec((th, d), lambda i, j: (j, 0)),
                  pl.BlockSpec((1, d), lambda i, j: (0, 0))],
        out_specs=pl.BlockSpec((tm, d), row),
        out_shape=jax.ShapeDtypeStruct((rows, d), F32),
        scratch_shapes=[pltpu.VMEM((tm, d), BF16)],
        compiler_params=pltpu.CompilerParams(
            dimension_semantics=("arbitrary", "arbitrary"), vmem_limit_bytes=VMEM_LIMIT),
        name="ffn",
    )(h, w["g_ffn"], w["w_gate"], w["w_up"], w["w_down"], w["g_final"])


def _rope_tables(pos, rot_dim):
    half = rot_dim // 2
    inv = ROPE_THETA ** (-jnp.arange(half, dtype=F32) * 2.0 / rot_dim)
    ang = pos.astype(F32)[:, None] * inv
    d = jnp.arange(LANES) % DIFF_HEAD_DIM
    a = ang[:, d % half]
    cos = jnp.where(d < rot_dim, jnp.cos(a), 1.0)
    sin = jnp.sin(a)
    sin_lo = jnp.where(d < half, -sin, 0.0)
    sin_hi = jnp.where((d >= half) & (d < rot_dim), sin, 0.0)
    return cos.astype(F32), sin_lo.astype(F32), sin_hi.astype(F32)


def _tables(pos):
    return _rope_tables(pos, MLA_ROPE) + _rope_tables(pos, DIFF_ROT)


def _row_block(rows, target):
    tm = min(rows, target)
    while rows % tm:
        tm //= 2
    return tm


def kernel(x_prompt, x_sample, cache_mla_latent, cache_mla_krope, cache_diff_k, cache_diff_v,
           page_table, meta_tokens, g_attn, w_in, g_q_a, w_q_b, g_kv_a, w_uk, w_uv, g_mla_out,
           lambda_q1, lambda_k1, lambda_q2, lambda_k2, g_subln, w_out, g_ffn, w_gate, w_up,
           w_down, g_final):
    batch, seq, d = x_prompt.shape
    n_seq, n_tok, _ = x_sample.shape
    depth = w_in.shape[0]
    assert depth == 1, "single-layer trunk"
    n_phys = cache_mla_latent.shape[1]
    past_len = page_table.shape[1] * PAGE_SIZE
    lam_init = 0.8 - 0.6 * math.exp(-0.3 * 0)

    wi = w_in[0]
    o = [0, MLA_Q_RANK, MLA_Q_RANK + MLA_KV_RANK, MLA_Q_RANK + MLA_KV_RANK + MLA_ROPE]
    o.append(o[3] + DIFF_WIDTH)
    o.append(o[4] + DIFF_KV_WIDTH)
    o.append(o[5] + DIFF_KV_WIDTH)
    w_in_r = jnp.concatenate(
        [wi[:, o[0]:o[2]], wi[:, o[3]:o[6]], wi[:, o[2]:o[3]],
         jnp.zeros((d, LANES - MLA_ROPE), wi.dtype)], axis=1).astype(BF16)
    wq = w_q_b[0]
    w_qb = jnp.concatenate(
        [wq[:, :, :MLA_QK_NOPE].reshape(MLA_Q_RANK, -1),
         wq[:, :, MLA_QK_NOPE:].reshape(MLA_Q_RANK, -1)], axis=1).astype(BF16)
    w = {
        "g_attn": g_attn[0][None], "w_in": w_in_r, "g_q_a": g_q_a[0][None],
        "g_kv_a": g_kv_a[0][None], "w_qb": w_qb,
        "w_uk": jnp.transpose(w_uk[0], (1, 2, 0)).astype(BF16),
        "w_uv": jnp.transpose(w_uv[0], (1, 0, 2)).astype(BF16),
        "g_mla_out": g_mla_out[0][None], "w_out": w_out[0].astype(BF16),
        "g_ffn": g_ffn[0][None], "w_gate": w_gate[0].astype(BF16),
        "w_up": w_up[0].astype(BF16), "w_down": w_down[0].astype(BF16),
        "g_final": g_final[None],
    }
    lams = (lambda_q1, lambda_k1, lambda_q2, lambda_k2)
    g_sub = g_subln[0][None]

    xp = x_prompt.reshape(batch * seq, d)
    xs = x_sample.reshape(n_seq * n_tok, d)
    tm_p = _row_block(seq, 256)
    tm_s = _row_block(n_seq * n_tok, 256)
    tab_p = _tables(N_META + jnp.arange(seq))
    tab_s = _tables(jnp.tile(past_len + jnp.arange(n_tok), tm_s // n_tok))
    tab_m = _tables(jnp.arange(N_META))
    (qabs_p, qpe_p, dq_p, c_p, kr_p, dk_p, dv_p, cb_p, krb_p, dkb_p, dvb_p) = _proj(
        xp, tab_p, tm_p, BF16, w)
    (qabs_s, qpe_s, dq_s, c_s, kr_s, dk_s, dv_s, _, _, _, _) = _proj(xs, tab_s, tm_s, F32, w)
    (_, _, _, c_m, kr_m, dk_m, dv_m, cb_m, krb_m, dkb_m, dvb_m) = _proj(
        meta_tokens.astype(F32), tab_m, N_META, BF16, w)

    lat_p = _mla_attn(qabs_p, qpe_p, cb_p, krb_p, cb_m, krb_m, batch, seq, _row_block(seq, 256))
    diff_p = _diff_attn(dq_p, dkb_p, dvb_p, dkb_m, dvb_m, lams, g_sub, batch, seq,
                        _row_block(seq, 128), _row_block(seq, 256), lam_init)

    pools = (cache_mla_latent,
             jnp.swapaxes(cache_mla_krope, 2, 3),
             jnp.transpose(cache_diff_k, (0, 1, 3, 4, 5, 2)).reshape(
                 depth, n_phys, DIFF_KV_WIDTH, PAGE_SIZE),
             cache_diff_v.reshape(depth, n_phys, PAGE_SIZE * DIFF_KV_HEADS, 2 * DIFF_HEAD_DIM))
    pages = _row_block(page_table.shape[1], 16)
    lat_s, diff_s = _paged_attn(page_table, qabs_s, qpe_s, dq_s, c_s, kr_s, dk_s, dv_s, lams,
                                g_sub, pools, n_tok, pages, lam_init)

    h_p = _out_proj(xp, lat_p, diff_p, w, _row_block(batch * seq, 512))
    h_s = _out_proj(xs, lat_s, diff_s, w, _row_block(n_seq * n_tok, 512))
    hidden = w_gate.shape[2]
    th = 512 if hidden % 512 == 0 else hidden
    y_p = _ffn(h_p, w, _row_block(batch * seq, 512), th).reshape(batch, seq, d)
    y_s = _ffn(h_s, w, _row_block(n_seq * n_tok, 512), th).reshape(n_seq, n_tok, d)

    def with_meta(meta, real):
        n = real.shape[-1]
        full = jnp.concatenate(
            [jnp.broadcast_to(meta[None], (batch, N_META, n)), real.reshape(batch, seq, n)], axis=1)
        return full[None]

    t = seq + N_META
    kv_shape = (DIFF_KV_HEADS, 2, DIFF_HEAD_DIM)
    v_shape = (DIFF_KV_HEADS, 2 * DIFF_HEAD_DIM)
    return (y_p, y_s,
            with_meta(c_m, c_p), with_meta(kr_m, kr_p),
            with_meta(dk_m, dk_p).reshape((1, batch, t) + kv_shape),
            with_meta(dv_m, dv_p).reshape((1, batch, t) + v_shape),
            c_s.reshape(1, n_seq, n_tok, -1), kr_s.reshape(1, n_seq, n_tok, -1),
            dk_s.reshape((1, n_seq, n_tok) + kv_shape),
            dv_s.reshape((1, n_seq, n_tok) + v_shape))
```

```python
import functools
import math

import jax
import jax.numpy as jnp
from jax import lax
from jax.experimental import pallas as pl
from jax.experimental.pallas import tpu as pltpu

F32 = jnp.float32
BF16 = jnp.bfloat16

N_META = 16
ROPE_THETA = 500000.0
NORM_EPS = 1e-6
SUBLN_EPS = 1e-5
NEG_INF = -1e30
PAGE_SIZE = 128

MLA_HEADS = 8
MLA_QK_NOPE = 128
MLA_ROPE = 64
MLA_V = 128
MLA_Q_RANK = 512
MLA_KV_RANK = 256
MLA_WIDTH = MLA_HEADS * MLA_V
MLA_SCALE = (MLA_QK_NOPE + MLA_ROPE) ** -0.5

DIFF_HEADS = 8
DIFF_KV_HEADS = 2
DIFF_REP = DIFF_HEADS // DIFF_KV_HEADS
DIFF_HEAD_DIM = 64
DIFF_ROT = DIFF_HEAD_DIM // 4
DIFF_WIDTH = DIFF_HEADS * 2 * DIFF_HEAD_DIM
DIFF_KV_WIDTH = DIFF_KV_HEADS * 2 * DIFF_HEAD_DIM
DIFF_SCALE = DIFF_HEAD_DIM ** -0.5
DIFF_ROWS = DIFF_KV_HEADS * DIFF_REP * 2

LANES = 128
VMEM_LIMIT = 56 * 1024 * 1024

OFF_QA = 0
OFF_KVA = OFF_QA + MLA_Q_RANK
OFF_DQ = OFF_KVA + MLA_KV_RANK
OFF_DK = OFF_DQ + DIFF_WIDTH
OFF_DV = OFF_DK + DIFF_KV_WIDTH
OFF_KR = OFF_DV + DIFF_KV_WIDTH
IN_WIDTH_PADDED = OFF_KR + LANES


def _rms(x, g, eps):
    return x * lax.rsqrt(jnp.mean(x * x, axis=-1, keepdims=True) + eps) * g


def _dot(a, b):
    return jnp.dot(a, b, preferred_element_type=F32)


def _dot_nt(a, b):
    return lax.dot_general(a, b, (((1,), (1,)), ((), ())), preferred_element_type=F32)


def _rope_chunk(x, cos, sin_lo, sin_hi, half):
    return (x * cos + pltpu.roll(x, LANES - half, 1) * sin_lo
            + pltpu.roll(x, half, 1) * sin_hi)


def _causal(s, n_tok, tok0, col0):
    assert n_tok & (n_tok - 1) == 0
    tok = tok0 + (lax.broadcasted_iota(jnp.int32, s.shape, 0) & (n_tok - 1))
    col = col0 + lax.broadcasted_iota(jnp.int32, s.shape, 1)
    return jnp.where(col <= tok, s, NEG_INF)


def _const_spec(shape):
    return pl.BlockSpec(shape, lambda *_: (0,) * len(shape))


def _proj_kernel(x_ref, g_attn_ref, w_in_ref, g_q_ref, g_kv_ref, w_qb_ref, w_uk_ref,
                 cos_m_ref, slo_m_ref, shi_m_ref, cos_d_ref, slo_d_ref, shi_d_ref,
                 qabs_ref, qpe_ref, dq_ref, c_ref, kr_ref, dk_ref, dv_ref,
                 cb_ref, krb_ref, dkb_ref, dvb_ref):
    xn = _rms(x_ref[...], g_attn_ref[...], NORM_EPS).astype(BF16)
    z = _dot(xn, w_in_ref[...])

    cos_m, slo_m, shi_m = cos_m_ref[...], slo_m_ref[...], shi_m_ref[...]
    cos_d, slo_d, shi_d = cos_d_ref[...], slo_d_ref[...], shi_d_ref[...]

    c = _rms(z[:, OFF_KVA:OFF_KVA + MLA_KV_RANK], g_kv_ref[...], NORM_EPS)
    c_ref[...] = c
    cb_ref[...] = c.astype(BF16)
    kr = _rope_chunk(z[:, OFF_KR:OFF_KR + LANES], cos_m, slo_m, shi_m, MLA_ROPE // 2)
    kr_ref[...] = kr[:, :MLA_ROPE]
    krb_ref[...] = kr[:, :MLA_ROPE].astype(BF16)

    qn = _rms(z[:, OFF_QA:OFF_QA + MLA_Q_RANK], g_q_ref[...], NORM_EPS).astype(BF16)
    q = _dot(qn, w_qb_ref[...])
    nope_w = MLA_HEADS * MLA_QK_NOPE
    for h in range(MLA_HEADS):
        qh = q[:, h * MLA_QK_NOPE:(h + 1) * MLA_QK_NOPE].astype(BF16)
        qabs_ref[:, h * MLA_KV_RANK:(h + 1) * MLA_KV_RANK] = _dot(qh, w_uk_ref[h]).astype(qabs_ref.dtype)
    for j in range(MLA_HEADS * MLA_ROPE // LANES):
        qc = q[:, nope_w + j * LANES:nope_w + (j + 1) * LANES]
        qpe_ref[:, j * LANES:(j + 1) * LANES] = _rope_chunk(
            qc, cos_m, slo_m, shi_m, MLA_ROPE // 2).astype(qpe_ref.dtype)

    for j in range(DIFF_WIDTH // LANES):
        xc = z[:, OFF_DQ + j * LANES:OFF_DQ + (j + 1) * LANES]
        dq_ref[:, j * LANES:(j + 1) * LANES] = _rope_chunk(
            xc, cos_d, slo_d, shi_d, DIFF_ROT // 2).astype(dq_ref.dtype)
    for j in range(DIFF_KV_WIDTH // LANES):
        xc = z[:, OFF_DK + j * LANES:OFF_DK + (j + 1) * LANES]
        dk = _rope_chunk(xc, cos_d, slo_d, shi_d, DIFF_ROT // 2)
        dk_ref[:, j * LANES:(j + 1) * LANES] = dk
        dkb_ref[:, j * LANES:(j + 1) * LANES] = dk.astype(BF16)
    dv = z[:, OFF_DV:OFF_DV + DIFF_KV_WIDTH]
    dv_ref[...] = dv
    dvb_ref[...] = dv.astype(BF16)


def _proj(x, tables, tm, q_dtype, w):
    rows, d = x.shape
    t_tab = tables[0].shape[0]
    n_tab = t_tab // tm
    row = lambda i: (i, 0)
    tab = lambda i: (i % n_tab, 0)
    out_widths = [(MLA_HEADS * MLA_KV_RANK, q_dtype), (MLA_HEADS * MLA_ROPE, q_dtype),
                  (DIFF_WIDTH, q_dtype), (MLA_KV_RANK, F32), (MLA_ROPE, F32),
                  (DIFF_KV_WIDTH, F32), (DIFF_KV_WIDTH, F32), (MLA_KV_RANK, BF16),
                  (MLA_ROPE, BF16), (DIFF_KV_WIDTH, BF16), (DIFF_KV_WIDTH, BF16)]
    return pl.pallas_call(
        _proj_kernel,
        grid=(rows // tm,),
        in_specs=[pl.BlockSpec((tm, d), row),
                  _const_spec((1, d)),
                  _const_spec(w["w_in"].shape),
                  _const_spec((1, MLA_Q_RANK)),
                  _const_spec((1, MLA_KV_RANK)),
                  _const_spec(w["w_qb"].shape),
                  _const_spec(w["w_uk"].shape)]
                 + [pl.BlockSpec((tm, LANES), tab)] * 6,
        out_specs=[pl.BlockSpec((tm, n), row) for n, _ in out_widths],
        out_shape=[jax.ShapeDtypeStruct((rows, n), dt) for n, dt in out_widths],
        compiler_params=pltpu.CompilerParams(
            dimension_semantics=("arbitrary",), vmem_limit_bytes=VMEM_LIMIT),
        name="proj",
    )(x, w["g_attn"], w["w_in"], w["g_q_a"], w["g_kv_a"], w["w_qb"], w["w_uk"], *tables)


def _t(x):
    return x.astype(F32).T


def _online_softmax_t(s, m_prev, l_prev, log2_scale):
    m_new = jnp.maximum(m_prev, jnp.max(s, axis=0, keepdims=True))
    alpha = jnp.exp2((m_prev - m_new) * log2_scale)
    p = jnp.exp2((s - m_new) * log2_scale)
    return p, alpha, m_new, alpha * l_prev + jnp.sum(p, axis=0, keepdims=True)


def _mla_attn_kernel(qabs_ref, qpe_ref, cb_ref, krb_ref, cmeta_ref, krmeta_ref, o_ref,
                     qt_ref, ct_ref, cm_ref, cmt_ref, m_ref, l_ref, acc_ref, s_ref, p_ref,
                     a_ref, *, bq):
    qi = pl.program_id(1)
    seq = cb_ref.shape[0]
    log2_scale = MLA_SCALE * math.log2(math.e)

    @pl.when(qi == 0)
    def _():
        for kb in range(seq // bq):
            ct_ref[kb] = _t(cb_ref[kb * bq:(kb + 1) * bq, :]).astype(BF16)
        cm_ref[...] = jnp.zeros(cm_ref.shape, BF16)
        cm_ref[0:N_META, :] = cmeta_ref[...]
        cmt_ref[...] = _t(cm_ref[...]).astype(BF16)

    for h in range(MLA_HEADS):
        qt_ref[h, 0:MLA_KV_RANK, :] = _t(
            qabs_ref[:, h * MLA_KV_RANK:(h + 1) * MLA_KV_RANK]).astype(BF16)
    per_chunk = LANES // MLA_ROPE
    for j in range(MLA_HEADS // per_chunk):
        t = _t(qpe_ref[:, j * LANES:(j + 1) * LANES]).astype(BF16)
        for i in range(per_chunk):
            qt_ref[j * per_chunk + i, MLA_KV_RANK:MLA_KV_RANK + MLA_ROPE, :] = (
                t[i * MLA_ROPE:(i + 1) * MLA_ROPE])
    for h in range(MLA_HEADS):
        s_ref[h, 0:N_META, :] = (
            _dot(cmeta_ref[...], qt_ref[h, 0:MLA_KV_RANK, :])
            + _dot(krmeta_ref[...], qt_ref[h, MLA_KV_RANK:MLA_KV_RANK + MLA_ROPE, :]))
    for h in range(MLA_HEADS):
        s = s_ref[h, 0:N_META, :]
        m0 = jnp.max(s, axis=0, keepdims=True)
        p = jnp.exp2((s - m0) * log2_scale)
        m_ref[h:h + 1, :] = m0
        l_ref[h:h + 1, :] = jnp.sum(p, axis=0, keepdims=True)
        p_ref[h, 0:N_META, :] = p.astype(BF16)
        p_ref[h, N_META:LANES, :] = jnp.zeros((LANES - N_META, bq), BF16)
    for h in range(MLA_HEADS):
        acc_ref[h] = _dot(cmt_ref[...], p_ref[h, 0:LANES, :])

    def attend(kc, kkr, ct, visible):
        n = kc.shape[0]
        for h in range(MLA_HEADS):
            s_ref[h, 0:n, :] = (_dot(kc, qt_ref[h, 0:MLA_KV_RANK, :])
                                + _dot(kkr, qt_ref[h, MLA_KV_RANK:MLA_KV_RANK + MLA_ROPE, :]))
        for h in range(MLA_HEADS):
            s = s_ref[h, 0:n, :]
            if visible is not None:
                s = jnp.where(visible, s, NEG_INF)
            p, alpha, m_new, l_new = _online_softmax_t(
                s, m_ref[h:h + 1, :], l_ref[h:h + 1, :], log2_scale)
            p_ref[h, 0:n, :] = p.astype(BF16)
            a_ref[h:h + 1, :] = alpha
            m_ref[h:h + 1, :] = m_new
            l_ref[h:h + 1, :] = l_new
        for h in range(MLA_HEADS):
            acc_ref[h] = a_ref[h:h + 1, :] * acc_ref[h] + _dot(ct, p_ref[h, 0:n, :])

    def body(kb, carry):
        start = pl.multiple_of(kb * bq, bq)
        attend(cb_ref[pl.ds(start, bq), :], krb_ref[pl.ds(start, bq), :], ct_ref[kb], None)
        return carry

    lax.fori_loop(0, qi, body, 0)
    start = pl.multiple_of(qi * bq, bq)
    key = lax.broadcasted_iota(jnp.int32, (bq, bq), 0)
    tok = lax.broadcasted_iota(jnp.int32, (bq, bq), 1)
    attend(cb_ref[pl.ds(start, bq), :], krb_ref[pl.ds(start, bq), :], ct_ref[qi], key <= tok)

    for h in range(MLA_HEADS):
        o_ref[:, h * MLA_KV_RANK:(h + 1) * MLA_KV_RANK] = (
            acc_ref[h] / l_ref[h:h + 1, :]).T.astype(o_ref.dtype)


def _mla_attn(qabs, qpe, cb, krb, cmeta, krmeta, batch, seq, bq):
    nq = seq // bq
    qrow = lambda b, i: (b * nq + i, 0)
    kv = lambda b, i: (b, 0)
    return pl.pallas_call(
        functools.partial(_mla_attn_kernel, bq=bq),
        grid=(batch, nq),
        in_specs=[pl.BlockSpec((bq, MLA_HEADS * MLA_KV_RANK), qrow),
                  pl.BlockSpec((bq, MLA_HEADS * MLA_ROPE), qrow),
                  pl.BlockSpec((seq, MLA_KV_RANK), kv),
                  pl.BlockSpec((seq, MLA_ROPE), kv),
                  _const_spec(cmeta.shape),
                  _const_spec(krmeta.shape)],
        out_specs=pl.BlockSpec((bq, MLA_HEADS * MLA_KV_RANK), qrow),
        out_shape=jax.ShapeDtypeStruct((batch * seq, MLA_HEADS * MLA_KV_RANK), BF16),
        scratch_shapes=[pltpu.VMEM((MLA_HEADS, MLA_KV_RANK + MLA_ROPE, bq), BF16),
                        pltpu.VMEM((nq, MLA_KV_RANK, bq), BF16),
                        pltpu.VMEM((LANES, MLA_KV_RANK), BF16),
                        pltpu.VMEM((MLA_KV_RANK, LANES), BF16),
                        pltpu.VMEM((MLA_HEADS, bq), F32),
                        pltpu.VMEM((MLA_HEADS, bq), F32),
                        pltpu.VMEM((MLA_HEADS, MLA_KV_RANK, bq), F32),
                        pltpu.VMEM((MLA_HEADS, bq, bq), F32),
                        pltpu.VMEM((MLA_HEADS, bq, bq), BF16),
                        pltpu.VMEM((MLA_HEADS, bq), F32)],
        compiler_params=pltpu.CompilerParams(
            dimension_semantics=("arbitrary", "arbitrary"), vmem_limit_bytes=VMEM_LIMIT),
        name="mla_attn",
    )(qabs, qpe, cb, krb, cmeta, krmeta)


def _lambda(lq1_ref, lk1_ref, lq2_ref, lk2_ref, lam_init):
    a = jnp.sum(lq1_ref[...] * lk1_ref[...], axis=-1, keepdims=True)
    b = jnp.sum(lq2_ref[...] * lk2_ref[...], axis=-1, keepdims=True)
    return jnp.exp(a) - jnp.exp(b) + lam_init


def _fill_diff_queries(qd_ref, dq_ref, n_tok):
    qd_ref[...] = jnp.zeros(qd_ref.shape, qd_ref.dtype)
    lane = lax.broadcasted_iota(jnp.int32, (n_tok, LANES), 1)
    for g in range(DIFF_KV_HEADS):
        for r in range(DIFF_REP):
            col = (g * DIFF_REP + r) * LANES
            x = dq_ref[:, col:col + LANES] * DIFF_SCALE
            for m in range(2):
                row = ((g * DIFF_REP + r) * 2 + m) * n_tok
                keep = (lane < DIFF_HEAD_DIM) if m == 0 else (lane >= DIFF_HEAD_DIM)
                qd_ref[row:row + n_tok, g * LANES:(g + 1) * LANES] = jnp.where(
                    keep, x, jnp.zeros_like(x)).astype(qd_ref.dtype)


def _finish_diff(o_ref, acc_ref, l_ref, lam, g_sub, lam_init, n_tok):
    for g in range(DIFF_KV_HEADS):
        for r in range(DIFF_REP):
            r0 = ((g * DIFF_REP + r) * 2) * n_tok
            r1 = r0 + n_tok
            o0 = acc_ref[r0:r0 + n_tok, :] / l_ref[r0:r0 + n_tok, :]
            o1 = acc_ref[r1:r1 + n_tok, :] / l_ref[r1:r1 + n_tok, :]
            d = _rms(o0 - lam * o1, g_sub, SUBLN_EPS) * (1.0 - lam_init)
            col = (g * DIFF_REP + r) * LANES
            o_ref[:, col:col + LANES] = d.astype(o_ref.dtype)


def _diff_attn_kernel(dq_ref, dkb_ref, dvb_ref, dkmeta_ref, dvmeta_ref,
                      lq1_ref, lk1_ref, lq2_ref, lk2_ref, g_sub_ref, o_ref,
                      qt_ref, vt_ref, vm_ref, vmt_ref, m_ref, l_ref, acc_ref,
                      s_ref, p_ref, a_ref, *, bq, bk, lam_init):
    qi = pl.program_id(1)
    seq = dkb_ref.shape[0]
    n_chunks = DIFF_KV_HEADS * DIFF_REP
    log2e = math.log2(math.e)

    @pl.when(qi == 0)
    def _():
        for kb in range(seq // bk):
            vt_ref[kb] = _t(dvb_ref[kb * bk:(kb + 1) * bk, :]).astype(BF16)
        vm_ref[...] = jnp.zeros(vm_ref.shape, BF16)
        vm_ref[0:N_META, :] = dvmeta_ref[...]
        vmt_ref[...] = _t(vm_ref[...]).astype(BF16)

    dim = lax.broadcasted_iota(jnp.int32, (LANES, bq), 0)
    for c in range(n_chunks):
        x = _t(dq_ref[:, c * LANES:(c + 1) * LANES] * DIFF_SCALE)
        qt_ref[c, :, 0:bq] = jnp.where(dim < DIFF_HEAD_DIM, x, 0.0).astype(BF16)
        qt_ref[c, :, bq:2 * bq] = jnp.where(dim >= DIFF_HEAD_DIM, x, 0.0).astype(BF16)

    for c in range(n_chunks):
        g = c // DIFF_REP
        s_ref[c, 0:N_META, :] = _dot(dkmeta_ref[:, g * LANES:(g + 1) * LANES], qt_ref[c])
    for c in range(n_chunks):
        s = s_ref[c, 0:N_META, :]
        m0 = jnp.max(s, axis=0, keepdims=True)
        p = jnp.exp2((s - m0) * log2e)
        m_ref[c:c + 1, :] = m0
        l_ref[c:c + 1, :] = jnp.sum(p, axis=0, keepdims=True)
        p_ref[c, 0:N_META, :] = p.astype(BF16)
        p_ref[c, N_META:LANES, :] = jnp.zeros((LANES - N_META, 2 * bq), BF16)
    for c in range(n_chunks):
        g = c // DIFF_REP
        acc_ref[c] = _dot(vmt_ref[g * LANES:(g + 1) * LANES, :], p_ref[c, 0:LANES, :])

    def attend(dk, vt, visible):
        n = dk.shape[0]
        for c in range(n_chunks):
            g = c // DIFF_REP
            s_ref[c, 0:n, :] = _dot(dk[:, g * LANES:(g + 1) * LANES], qt_ref[c])
        for c in range(n_chunks):
            s = s_ref[c, 0:n, :]
            if visible is not None:
                s = jnp.where(visible, s, NEG_INF)
            p, alpha, m_new, l_new = _online_softmax_t(
                s, m_ref[c:c + 1, :], l_ref[c:c + 1, :], log2e)
            p_ref[c, 0:n, :] = p.astype(BF16)
            a_ref[c:c + 1, :] = alpha
            m_ref[c:c + 1, :] = m_new
            l_ref[c:c + 1, :] = l_new
        for c in range(n_chunks):
            g = c // DIFF_REP
            acc_ref[c] = a_ref[c:c + 1, :] * acc_ref[c] + _dot(
                vt[g * LANES:(g + 1) * LANES, :], p_ref[c, 0:n, :])

    n_full = (qi * bq) // bk

    def body(kb, carry):
        start = pl.multiple_of(kb * bk, bk)
        attend(dkb_ref[pl.ds(start, bk), :], vt_ref[kb], None)
        return carry

    lax.fori_loop(0, n_full, body, 0)
    start = pl.multiple_of(n_full * bk, bk)
    key = start + lax.broadcasted_iota(jnp.int32, (bk, 2 * bq), 0)
    tok = qi * bq + (lax.broadcasted_iota(jnp.int32, (bk, 2 * bq), 1) & (bq - 1))
    attend(dkb_ref[pl.ds(start, bk), :], vt_ref[n_full], key <= tok)

    lam = _lambda(lq1_ref, lk1_ref, lq2_ref, lk2_ref, lam_init)
    for c in range(n_chunks):
        o = acc_ref[c] / l_ref[c:c + 1, :]
        d = (o[:, 0:bq] - lam * o[:, bq:2 * bq]).T
        d = _rms(d, g_sub_ref[...], SUBLN_EPS) * (1.0 - lam_init)
        o_ref[:, c * LANES:(c + 1) * LANES] = d.astype(o_ref.dtype)


def _diff_attn(dq, dkb, dvb, dkmeta, dvmeta, lams, g_sub, batch, seq, bq, bk, lam_init):
    assert bq & (bq - 1) == 0 and bk % bq == 0 and seq % bk == 0
    nq = seq // bq
    n_chunks = DIFF_KV_HEADS * DIFF_REP
    qrow = lambda b, i: (b * nq + i, 0)
    kv = lambda b, i: (b, 0)
    return pl.pallas_call(
        functools.partial(_diff_attn_kernel, bq=bq, bk=bk, lam_init=lam_init),
        grid=(batch, nq),
        in_specs=[pl.BlockSpec((bq, DIFF_WIDTH), qrow),
                  pl.BlockSpec((seq, DIFF_KV_WIDTH), kv),
                  pl.BlockSpec((seq, DIFF_KV_WIDTH), kv),
                  _const_spec(dkmeta.shape),
                  _const_spec(dvmeta.shape)]
                 + [_const_spec((1, DIFF_HEAD_DIM))] * 4
                 + [_const_spec((1, 2 * DIFF_HEAD_DIM))],
        out_specs=pl.BlockSpec((bq, DIFF_WIDTH), qrow),
        out_shape=jax.ShapeDtypeStruct((batch * seq, DIFF_WIDTH), BF16),
        scratch_shapes=[pltpu.VMEM((n_chunks, LANES, 2 * bq), BF16),
                        pltpu.VMEM((seq // bk, DIFF_KV_WIDTH, bk), BF16),
                        pltpu.VMEM((LANES, DIFF_KV_WIDTH), BF16),
                        pltpu.VMEM((DIFF_KV_WIDTH, LANES), BF16),
                        pltpu.VMEM((n_chunks, 2 * bq), F32),
                        pltpu.VMEM((n_chunks, 2 * bq), F32),
                        pltpu.VMEM((n_chunks, 2 * DIFF_HEAD_DIM, 2 * bq), F32),
                        pltpu.VMEM((n_chunks, bk, 2 * bq), F32),
                        pltpu.VMEM((n_chunks, bk, 2 * bq), BF16),
                        pltpu.VMEM((n_chunks, 2 * bq), F32)],
        compiler_params=pltpu.CompilerParams(
            dimension_semantics=("arbitrary", "arbitrary"), vmem_limit_bytes=VMEM_LIMIT),
        name="diff_attn",
    )(dq, dkb, dvb, dkmeta, dvmeta, *lams, g_sub)


def _paged_attn_kernel(pt_ref, qabs_ref, qpe_ref, dq_ref, c_new_ref, kr_new_ref, dk_new_ref,
                       dv_new_ref, lq1_ref, lk1_ref, lq2_ref, lk2_ref, g_sub_ref,
                       lat_hbm, krt_hbm, dkt_hbm, dv_hbm, mla_o_ref, diff_o_ref,
                       qm_ref, qp_ref, qd_ref, kc_ref, krt_ref, dkt_ref, dv_ref, new_c_ref,
                       new_kr_ref, new_dk_ref, new_dv_ref, mm_ref, lm_ref, accm_ref, md_ref,
                       ld_ref, accd_ref, lat_buf, krt_buf, dkt_buf, dv_buf, sem,
                       *, n_tok, pages, lam_init):
    b = pl.program_id(0)
    j = pl.program_id(1)
    n_steps = pl.num_programs(1)
    half = DIFF_ROWS // DIFF_KV_HEADS * n_tok

    t = b * n_steps + j
    slot = lax.rem(t, 2)
    pools = ((lat_hbm, lat_buf), (krt_hbm, krt_buf), (dkt_hbm, dkt_buf), (dv_hbm, dv_buf))

    def page_copies(seq, step, into):
        return [pltpu.make_async_copy(src.at[0, pt_ref[seq, step * pages + p_]],
                                      dst.at[into, p_], sem.at[into])
                for p_ in range(pages) for src, dst in pools]

    @pl.when(t == 0)
    def _():
        for copy in page_copies(b, j, slot):
            copy.start()

    @pl.when(t + 1 < pl.num_programs(0) * n_steps)
    def _():
        wrap = j + 1 == n_steps
        for copy in page_copies(jnp.where(wrap, b + 1, b), jnp.where(wrap, 0, j + 1), 1 - slot):
            copy.start()

    for copy in page_copies(b, j, slot):
        copy.wait()
    page_refs = [buf.at[slot, p_] for p_ in range(pages) for _, buf in pools]

    def update(s_m, s_d, mix_m, mix_d):
        m_prev = mm_ref[...]
        m_new = jnp.maximum(m_prev, jnp.max(s_m, axis=-1, keepdims=True))
        alpha = jnp.exp(m_prev - m_new)
        p = jnp.exp(s_m - m_new)
        lm_ref[...] = alpha * lm_ref[...] + jnp.sum(p, axis=-1, keepdims=True)
        accm_ref[...] = alpha * accm_ref[...] + mix_m(p.astype(BF16))
        mm_ref[...] = m_new
        m_prev = md_ref[...]
        m_new = jnp.maximum(m_prev, jnp.max(s_d, axis=-1, keepdims=True))
        alpha = jnp.exp(m_prev - m_new)
        p = jnp.exp(s_d - m_new)
        ld_ref[...] = alpha * ld_ref[...] + jnp.sum(p, axis=-1, keepdims=True)
        p = p.astype(BF16)
        for g in range(DIFF_KV_HEADS):
            rows = slice(g * half, (g + 1) * half)
            accd_ref[rows, :] = alpha[rows] * accd_ref[rows, :] + mix_d(p[rows], g)
        md_ref[...] = m_new

    @pl.when(j == 0)
    def _():
        for h in range(MLA_HEADS):
            qm_ref[h * n_tok:(h + 1) * n_tok, :] = qabs_ref[:, h * MLA_KV_RANK:(h + 1) * MLA_KV_RANK]
            qp_ref[h * n_tok:(h + 1) * n_tok, :] = qpe_ref[:, h * MLA_ROPE:(h + 1) * MLA_ROPE]
        _fill_diff_queries(qd_ref, dq_ref, n_tok)
        mm_ref[...] = jnp.full(mm_ref.shape, NEG_INF, F32)
        lm_ref[...] = jnp.zeros(lm_ref.shape, F32)
        accm_ref[...] = jnp.zeros(accm_ref.shape, F32)
        md_ref[...] = jnp.full(md_ref.shape, NEG_INF, F32)
        ld_ref[...] = jnp.zeros(ld_ref.shape, F32)
        accd_ref[...] = jnp.zeros(accd_ref.shape, F32)
        for dst, src in ((new_c_ref, c_new_ref), (new_kr_ref, kr_new_ref),
                         (new_dk_ref, dk_new_ref), (new_dv_ref, dv_new_ref)):
            dst[...] = jnp.zeros(dst.shape, F32)
            dst[0:n_tok, :] = src[...]
        new_c = new_c_ref[...].astype(BF16)
        new_dv = new_dv_ref[...].astype(BF16)
        s_m = (_dot_nt(qm_ref[...].astype(BF16), new_c)
               + _dot_nt(qp_ref[...].astype(BF16), new_kr_ref[...].astype(BF16))) * MLA_SCALE
        s_d = _dot_nt(qd_ref[...].astype(BF16), new_dk_ref[...].astype(BF16))
        update(_causal(s_m, n_tok, 0, 0), _causal(s_d, n_tok, 0, 0),
               lambda p: _dot(p, new_c),
               lambda p, g: _dot(p, new_dv[:, g * LANES:(g + 1) * LANES]))

    for p_ in range(pages):
        span = slice(p_ * PAGE_SIZE, (p_ + 1) * PAGE_SIZE)
        kc_ref[span, :] = page_refs[4 * p_][...].astype(BF16)
        krt_ref[:, span] = page_refs[4 * p_ + 1][...].astype(BF16)
        dkt_ref[:, span] = page_refs[4 * p_ + 2][...].astype(BF16)
        for g in range(DIFF_KV_HEADS):
            dv_ref[g, span, :] = page_refs[4 * p_ + 3][
                pl.ds(g, PAGE_SIZE, stride=DIFF_KV_HEADS), :].astype(BF16)
    s_m = (_dot_nt(qm_ref[...].astype(BF16), kc_ref[...])
           + _dot(qp_ref[...].astype(BF16), krt_ref[...])) * MLA_SCALE
    s_d = _dot(qd_ref[...].astype(BF16), dkt_ref[...])
    update(s_m, s_d, lambda p: _dot(p, kc_ref[...]), lambda p, g: _dot(p, dv_ref[g]))

    @pl.when(j == pl.num_programs(1) - 1)
    def _():
        for h in range(MLA_HEADS):
            rows = slice(h * n_tok, (h + 1) * n_tok)
            mla_o_ref[:, h * MLA_KV_RANK:(h + 1) * MLA_KV_RANK] = accm_ref[rows, :] / lm_ref[rows, :]
        lam = _lambda(lq1_ref, lk1_ref, lq2_ref, lk2_ref, lam_init)
        _finish_diff(diff_o_ref, accd_ref, ld_ref, lam, g_sub_ref[...], lam_init, n_tok)


def _paged_attn(page_table, qabs, qpe, dq, c_new, kr_new, dk_new, dv_new, lams, g_sub,
                pools, n_tok, pages, lam_init):
    n_seq, n_pages = page_table.shape
    seq_row = lambda b, j, pt: (b, 0)
    const = lambda b, j, pt: (0, 0)
    page_specs = [pl.BlockSpec(memory_space=pl.ANY)] * len(pools)
    page_bufs = [pltpu.VMEM((2, pages) + pool.shape[2:], pool.dtype) for pool in pools]
    n_keys = pages * PAGE_SIZE
    mla_rows = MLA_HEADS * n_tok
    diff_rows = DIFF_ROWS * n_tok
    grid_spec = pltpu.PrefetchScalarGridSpec(
        num_scalar_prefetch=1,
        grid=(n_seq, n_pages // pages),
        in_specs=[pl.BlockSpec((n_tok, MLA_HEADS * MLA_KV_RANK), seq_row),
                  pl.BlockSpec((n_tok, MLA_HEADS * MLA_ROPE), seq_row),
                  pl.BlockSpec((n_tok, DIFF_WIDTH), seq_row),
                  pl.BlockSpec((n_tok, MLA_KV_RANK), seq_row),
                  pl.BlockSpec((n_tok, MLA_ROPE), seq_row),
                  pl.BlockSpec((n_tok, DIFF_KV_WIDTH), seq_row),
                  pl.BlockSpec((n_tok, DIFF_KV_WIDTH), seq_row)]
                 + [pl.BlockSpec((1, DIFF_HEAD_DIM), const)] * 4
                 + [pl.BlockSpec((1, 2 * DIFF_HEAD_DIM), const)]
                 + page_specs,
        out_specs=[pl.BlockSpec((n_tok, MLA_HEADS * MLA_KV_RANK), seq_row),
                   pl.BlockSpec((n_tok, DIFF_WIDTH), seq_row)],
        scratch_shapes=[pltpu.VMEM((mla_rows, MLA_KV_RANK), F32),
                        pltpu.VMEM((mla_rows, MLA_ROPE), F32),
                        pltpu.VMEM((diff_rows, DIFF_KV_WIDTH), F32),
                        pltpu.VMEM((n_keys, MLA_KV_RANK), BF16),
                        pltpu.VMEM((MLA_ROPE, n_keys), BF16),
                        pltpu.VMEM((DIFF_KV_WIDTH, n_keys), BF16),
                        pltpu.VMEM((DIFF_KV_HEADS, n_keys, 2 * DIFF_HEAD_DIM), BF16),
                        pltpu.VMEM((PAGE_SIZE, MLA_KV_RANK), F32),
                        pltpu.VMEM((PAGE_SIZE, MLA_ROPE), F32),
                        pltpu.VMEM((PAGE_SIZE, DIFF_KV_WIDTH), F32),
                        pltpu.VMEM((PAGE_SIZE, DIFF_KV_WIDTH), F32),
                        pltpu.VMEM((mla_rows, 1), F32),
                        pltpu.VMEM((mla_rows, 1), F32),
                        pltpu.VMEM((mla_rows, MLA_KV_RANK), F32),
                        pltpu.VMEM((diff_rows, 1), F32),
                        pltpu.VMEM((diff_rows, 1), F32),
                        pltpu.VMEM((diff_rows, 2 * DIFF_HEAD_DIM), F32)]
                       + page_bufs + [pltpu.SemaphoreType.DMA((2,))],
    )
    rows = n_seq * n_tok
    return pl.pallas_call(
        functools.partial(_paged_attn_kernel, n_tok=n_tok, pages=pages, lam_init=lam_init),
        grid_spec=grid_spec,
        out_shape=[jax.ShapeDtypeStruct((rows, MLA_HEADS * MLA_KV_RANK), F32),
                   jax.ShapeDtypeStruct((rows, DIFF_WIDTH), F32)],
        compiler_params=pltpu.CompilerParams(
            dimension_semantics=("arbitrary", "arbitrary"), vmem_limit_bytes=VMEM_LIMIT),
        name="paged_attn",
    )(page_table, qabs, qpe, dq, c_new, kr_new, dk_new, dv_new, *lams, g_sub, *pools)


def _out_proj_kernel(x_ref, lat_ref, diff_ref, w_uv_ref, g_mla_ref, w_out_ref, h_ref, mla_ref):
    for h in range(MLA_HEADS):
        lat = lat_ref[:, h * MLA_KV_RANK:(h + 1) * MLA_KV_RANK].astype(BF16)
        mla_ref[:, h * MLA_V:(h + 1) * MLA_V] = _dot(lat, w_uv_ref[h])
    mla = _rms(mla_ref[...], g_mla_ref[...], NORM_EPS).astype(BF16)
    h_ref[...] = (x_ref[...] + _dot(mla, w_out_ref[0:MLA_WIDTH, :])
                  + _dot(diff_ref[...].astype(BF16), w_out_ref[MLA_WIDTH:MLA_WIDTH + DIFF_WIDTH, :]))


def _out_proj(x, lat, diff, w, tm):
    rows, d = x.shape
    row = lambda i: (i, 0)
    return pl.pallas_call(
        _out_proj_kernel,
        grid=(rows // tm,),
        in_specs=[pl.BlockSpec((tm, d), row),
                  pl.BlockSpec((tm, lat.shape[1]), row),
                  pl.BlockSpec((tm, diff.shape[1]), row),
                  _const_spec(w["w_uv"].shape),
                  _const_spec((1, MLA_WIDTH)),
                  _const_spec(w["w_out"].shape)],
        out_specs=pl.BlockSpec((tm, d), row),
        out_shape=jax.ShapeDtypeStruct((rows, d), F32),
        scratch_shapes=[pltpu.VMEM((tm, MLA_WIDTH), F32)],
        compiler_params=pltpu.CompilerParams(
            dimension_semantics=("arbitrary",), vmem_limit_bytes=VMEM_LIMIT),
        name="out_proj",
    )(x, lat, diff, w["w_uv"], w["g_mla_out"], w["w_out"])


def _ffn_kernel(h_ref, g_ffn_ref, w_gate_ref, w_up_ref, w_down_ref, g_final_ref, o_ref, xn_ref):
    j = pl.program_id(1)

    @pl.when(j == 0)
    def _():
        h = h_ref[...]
        xn_ref[...] = _rms(h, g_ffn_ref[...], NORM_EPS).astype(BF16)
        o_ref[...] = h

    xn = xn_ref[...]
    gate = _dot(xn, w_gate_ref[...])
    up = _dot(xn, w_up_ref[...])
    act = (gate * (1.0 / (1.0 + jnp.exp(-gate))) * up).astype(BF16)
    o_ref[...] += _dot(act, w_down_ref[...])

    @pl.when(j == pl.num_programs(1) - 1)
    def _():
        o_ref[...] = _rms(o_ref[...], g_final_ref[...], NORM_EPS)


def _ffn(h, w, tm, th):
    rows, d = h.shape
    hidden = w["w_gate"].shape[1]
    row = lambda i, j: (i, 0)
    return pl.pallas_call(
        _ffn_kernel,
        grid=(rows // tm, hidden // th),
        in_specs=[pl.BlockSpec((tm, d), row),
                  pl.BlockSpec((1, d), lambda i, j: (0, 0)),
                  pl.BlockSpec((d, th), lambda i, j: (0, j)),
                  pl.BlockSpec((d, th), lambda i, j: (0, j)),
                  pl.BlockSpec((th, d), lambda i, j: (j, 0)),
                  pl.BlockSpec((1, d), lambda i, j: (0, 0))],
        out_specs=pl.BlockSpec((tm, d), row),
        out_shape=jax.ShapeDtypeStruct((rows, d), F32),
        scratch_shapes=[pltpu.VMEM((tm, d), BF16)],
        compiler_params=pltpu.CompilerParams(
            dimension_semantics=("arbitrary", "arbitrary"), vmem_limit_bytes=VMEM_LIMIT),
        name="ffn",
    )(h, w["g_ffn"], w["w_gate"], w["w_up"], w["w_down"], w["g_final"])


def _rope_tables(pos, rot_dim):
    half = rot_dim // 2
    inv = ROPE_THETA ** (-jnp.arange(half, dtype=F32) * 2.0 / rot_dim)
    ang = pos.astype(F32)[:, None] * inv
    d = jnp.arange(LANES) % DIFF_HEAD_DIM
    a = ang[:, d % half]
    cos = jnp.where(d < rot_dim, jnp.cos(a), 1.0)
    sin = jnp.sin(a)
    sin_lo = jnp.where(d < half, -sin, 0.0)
    sin_hi = jnp.where((d >= half) & (d < rot_dim), sin, 0.0)
    return cos.astype(F32), sin_lo.astype(F32), sin_hi.astype(F32)


def _tables(pos):
    return _rope_tables(pos, MLA_ROPE) + _rope_tables(pos, DIFF_ROT)


def _row_block(rows, target):
    tm = min(rows, target)
    while rows % tm:
        tm //= 2
    return tm


def kernel(x_prompt, x_sample, cache_mla_latent, cache_mla_krope, cache_diff_k, cache_diff_v,
           page_table, meta_tokens, g_attn, w_in, g_q_a, w_q_b, g_kv_a, w_uk, w_uv, g_mla_out,
           lambda_q1, lambda_k1, lambda_q2, lambda_k2, g_subln, w_out, g_ffn, w_gate, w_up,
           w_down, g_final):
    batch, seq, d = x_prompt.shape
    n_seq, n_tok, _ = x_sample.shape
    depth = w_in.shape[0]
    assert depth == 1, "single-layer trunk"
    n_phys = cache_mla_latent.shape[1]
    past_len = page_table.shape[1] * PAGE_SIZE
    lam_init = 0.8 - 0.6 * math.exp(-0.3 * 0)

    wi = w_in[0]
    o = [0, MLA_Q_RANK, MLA_Q_RANK + MLA_KV_RANK, MLA_Q_RANK + MLA_KV_RANK + MLA_ROPE]
    o.append(o[3] + DIFF_WIDTH)
    o.append(o[4] + DIFF_KV_WIDTH)
    o.append(o[5] + DIFF_KV_WIDTH)
    w_in_r = jnp.concatenate(
        [wi[:, o[0]:o[2]], wi[:, o[3]:o[6]], wi[:, o[2]:o[3]],
         jnp.zeros((d, LANES - MLA_ROPE), wi.dtype)], axis=1).astype(BF16)
    wq = w_q_b[0]
    w_qb = jnp.concatenate(
        [wq[:, :, :MLA_QK_NOPE].reshape(MLA_Q_RANK, -1),
         wq[:, :, MLA_QK_NOPE:].reshape(MLA_Q_RANK, -1)], axis=1).astype(BF16)
    w = {
        "g_attn": g_attn[0][None], "w_in": w_in_r, "g_q_a": g_q_a[0][None],
        "g_kv_a": g_kv_a[0][None], "w_qb": w_qb,
        "w_uk": jnp.transpose(w_uk[0], (1, 2, 0)).astype(BF16),
        "w_uv": jnp.transpose(w_uv[0], (1, 0, 2)).astype(BF16),
        "g_mla_out": g_mla_out[0][None], "w_out": w_out[0].astype(BF16),
        "g_ffn": g_ffn[0][None], "w_gate": w_gate[0].astype(BF16),
        "w_up": w_up[0].astype(BF16), "w_down": w_down[0].astype(BF16),
        "g_final": g_final[None],
    }
    lams = (lambda_q1, lambda_k1, lambda_q2, lambda_k2)
    g_sub = g_subln[0][None]

    xp = x_prompt.reshape(batch * seq, d)
    xs = x_sample.reshape(n_seq * n_tok, d)
    tm_p = _row_block(seq, 256)
    tm_s = _row_block(n_seq * n_tok, 256)
    tab_p = _tables(N_META + jnp.arange(seq))
    tab_s = _tables(jnp.tile(past_len + jnp.arange(n_tok), tm_s // n_tok))
    tab_m = _tables(jnp.arange(N_META))
    (qabs_p, qpe_p, dq_p, c_p, kr_p, dk_p, dv_p, cb_p, krb_p, dkb_p, dvb_p) = _proj(
        xp, tab_p, tm_p, BF16, w)
    (qabs_s, qpe_s, dq_s, c_s, kr_s, dk_s, dv_s, _, _, _, _) = _proj(xs, tab_s, tm_s, F32, w)
    (_, _, _, c_m, kr_m, dk_m, dv_m, cb_m, krb_m, dkb_m, dvb_m) = _proj(
        meta_tokens.astype(F32), tab_m, N_META, BF16, w)

    lat_p = _mla_attn(qabs_p, qpe_p, cb_p, krb_p, cb_m, krb_m, batch, seq, _row_block(seq, 256))
    diff_p = _diff_attn(dq_p, dkb_p, dvb_p, dkb_m, dvb_m, lams, g_sub, batch, seq,
                        _row_block(seq, 128), _row_block(seq, 256), lam_init)

    pools = (cache_mla_latent,
             jnp.swapaxes(cache_mla_krope, 2, 3),
             jnp.transpose(cache_diff_k, (0, 1, 3, 4, 5, 2)).reshape(
                 depth, n_phys, DIFF_KV_WIDTH, PAGE_SIZE),
             cache_diff_v.reshape(depth, n_phys, PAGE_SIZE * DIFF_KV_HEADS, 2 * DIFF_HEAD_DIM))
    pages = _row_block(page_table.shape[1], 16)
    lat_s, diff_s = _paged_attn(page_table, qabs_s, qpe_s, dq_s, c_s, kr_s, dk_s, dv_s, lams,
                                g_sub, pools, n_tok, pages, lam_init)

    h_p = _out_proj(xp, lat_p, diff_p, w, _row_block(batch * seq, 512))
    h_s = _out_proj(xs, lat_s, diff_s, w, _row_block(n_seq * n_tok, 512))
    hidden = w_gate.shape[2]
    th = 512 if hidden % 512 == 0 else hidden
    y_p = _ffn(h_p, w, _row_block(batch * seq, 512), th).reshape(batch, seq, d)
    y_s = _ffn(h_s, w, _row_block(n_seq * n_tok, 512), th).reshape(n_seq, n_tok, d)

    def with_meta(meta, real):
        n = real.shape[-1]
        full = jnp.concatenate(
            [jnp.broadcast_to(meta[None], (batch, N_META, n)), real.reshape(batch, seq, n)], axis=1)
        return full[None]

    t = seq + N_META
    kv_shape = (DIFF_KV_HEADS, 2, DIFF_HEAD_DIM)
    v_shape = (DIFF_KV_HEADS, 2 * DIFF_HEAD_DIM)
    return (y_p, y_s,
            with_meta(c_m, c_p), with_meta(kr_m, kr_p),
            with_meta(dk_m, dk_p).reshape((1, batch, t) + kv_shape),
            with_meta(dv_m, dv_p).reshape((1, batch, t) + v_shape),
            c_s.reshape(1, n_seq, n_tok, -1), kr_s.reshape(1, n_seq, n_tok, -1),
            dk_s.reshape((1, n_seq, n_tok) + kv_shape),
            dv_s.reshape((1, n_seq, n_tok) + v_shape))
```

```python
import functools
import math

import jax
import jax.numpy as jnp
from jax import lax
from jax.experimental import pallas as pl
from jax.experimental.pallas import tpu as pltpu

F32 = jnp.float32
BF16 = jnp.bfloat16

N_META = 16
ROPE_THETA = 500000.0
NORM_EPS = 1e-6
SUBLN_EPS = 1e-5
NEG_INF = -1e30
PAGE_SIZE = 128

MLA_HEADS = 8
MLA_QK_NOPE = 128
MLA_ROPE = 64
MLA_V = 128
MLA_Q_RANK = 512
MLA_KV_RANK = 256
MLA_WIDTH = MLA_HEADS * MLA_V
MLA_SCALE = (MLA_QK_NOPE + MLA_ROPE) ** -0.5

DIFF_HEADS = 8
DIFF_KV_HEADS = 2
DIFF_REP = DIFF_HEADS // DIFF_KV_HEADS
DIFF_HEAD_DIM = 64
DIFF_ROT = DIFF_HEAD_DIM // 4
DIFF_WIDTH = DIFF_HEADS * 2 * DIFF_HEAD_DIM
DIFF_KV_WIDTH = DIFF_KV_HEADS * 2 * DIFF_HEAD_DIM
DIFF_SCALE = DIFF_HEAD_DIM ** -0.5
DIFF_ROWS = DIFF_KV_HEADS * DIFF_REP * 2

PAGE_SLOTS = 3
LANES = 128
VMEM_LIMIT = 56 * 1024 * 1024

OFF_QA = 0
OFF_KVA = OFF_QA + MLA_Q_RANK
OFF_DQ = OFF_KVA + MLA_KV_RANK
OFF_DK = OFF_DQ + DIFF_WIDTH
OFF_DV = OFF_DK + DIFF_KV_WIDTH
OFF_KR = OFF_DV + DIFF_KV_WIDTH
IN_WIDTH_PADDED = OFF_KR + LANES


def _rms(x, g, eps):
    return x * lax.rsqrt(jnp.mean(x * x, axis=-1, keepdims=True) + eps) * g


def _dot(a, b):
    return jnp.dot(a, b, preferred_element_type=F32)


def _dot_nt(a, b):
    return lax.dot_general(a, b, (((1,), (1,)), ((), ())), preferred_element_type=F32)


def _rope_chunk(x, cos, sin_lo, sin_hi, half):
    return (x * cos + pltpu.roll(x, LANES - half, 1) * sin_lo
            + pltpu.roll(x, half, 1) * sin_hi)


def _causal(s, n_tok, tok0, col0):
    assert n_tok & (n_tok - 1) == 0
    tok = tok0 + (lax.broadcasted_iota(jnp.int32, s.shape, 0) & (n_tok - 1))
    col = col0 + lax.broadcasted_iota(jnp.int32, s.shape, 1)
    return jnp.where(col <= tok, s, NEG_INF)


def _const_spec(shape):
    return pl.BlockSpec(shape, lambda *_: (0,) * len(shape))


def _proj_kernel(x_ref, g_attn_ref, w_in_ref, g_q_ref, g_kv_ref, w_qb_ref, w_uk_ref,
                 cos_m_ref, slo_m_ref, shi_m_ref, cos_d_ref, slo_d_ref, shi_d_ref,
                 qabs_ref, qpe_ref, dq_ref, c_ref, kr_ref, dk_ref, dv_ref,
                 cb_ref, krb_ref, dkb_ref, dvb_ref):
    xn = _rms(x_ref[...], g_attn_ref[...], NORM_EPS).astype(BF16)
    z = _dot(xn, w_in_ref[...])

    cos_m, slo_m, shi_m = cos_m_ref[...], slo_m_ref[...], shi_m_ref[...]
    cos_d, slo_d, shi_d = cos_d_ref[...], slo_d_ref[...], shi_d_ref[...]

    c = _rms(z[:, OFF_KVA:OFF_KVA + MLA_KV_RANK], g_kv_ref[...], NORM_EPS)
    c_ref[...] = c
    cb_ref[...] = c.astype(BF16)
    kr = _rope_chunk(z[:, OFF_KR:OFF_KR + LANES], cos_m, slo_m, shi_m, MLA_ROPE // 2)
    kr_ref[...] = kr[:, :MLA_ROPE]
    krb_ref[...] = kr[:, :MLA_ROPE].astype(BF16)

    qn = _rms(z[:, OFF_QA:OFF_QA + MLA_Q_RANK], g_q_ref[...], NORM_EPS).astype(BF16)
    q = _dot(qn, w_qb_ref[...])
    nope_w = MLA_HEADS * MLA_QK_NOPE
    for h in range(MLA_HEADS):
        qh = q[:, h * MLA_QK_NOPE:(h + 1) * MLA_QK_NOPE].astype(BF16)
        qabs_ref[:, h * MLA_KV_RANK:(h + 1) * MLA_KV_RANK] = _dot(qh, w_uk_ref[h]).astype(qabs_ref.dtype)
    for j in range(MLA_HEADS * MLA_ROPE // LANES):
        qc = q[:, nope_w + j * LANES:nope_w + (j + 1) * LANES]
        qpe_ref[:, j * LANES:(j + 1) * LANES] = _rope_chunk(
            qc, cos_m, slo_m, shi_m, MLA_ROPE // 2).astype(qpe_ref.dtype)

    for j in range(DIFF_WIDTH // LANES):
        xc = z[:, OFF_DQ + j * LANES:OFF_DQ + (j + 1) * LANES]
        dq_ref[:, j * LANES:(j + 1) * LANES] = _rope_chunk(
            xc, cos_d, slo_d, shi_d, DIFF_ROT // 2).astype(dq_ref.dtype)
    for j in range(DIFF_KV_WIDTH // LANES):
        xc = z[:, OFF_DK + j * LANES:OFF_DK + (j + 1) * LANES]
        dk = _rope_chunk(xc, cos_d, slo_d, shi_d, DIFF_ROT // 2)
        dk_ref[:, j * LANES:(j + 1) * LANES] = dk
        dkb_ref[:, j * LANES:(j + 1) * LANES] = dk.astype(BF16)
    dv = z[:, OFF_DV:OFF_DV + DIFF_KV_WIDTH]
    dv_ref[...] = dv
    dvb_ref[...] = dv.astype(BF16)


def _proj(x, tables, tm, q_dtype, w):
    rows, d = x.shape
    t_tab = tables[0].shape[0]
    n_tab = t_tab // tm
    row = lambda i: (i, 0)
    tab = lambda i: (i % n_tab, 0)
    out_widths = [(MLA_HEADS * MLA_KV_RANK, q_dtype), (MLA_HEADS * MLA_ROPE, q_dtype),
                  (DIFF_WIDTH, q_dtype), (MLA_KV_RANK, F32), (MLA_ROPE, F32),
                  (DIFF_KV_WIDTH, F32), (DIFF_KV_WIDTH, F32), (MLA_KV_RANK, BF16),
                  (MLA_ROPE, BF16), (DIFF_KV_WIDTH, BF16), (DIFF_KV_WIDTH, BF16)]
    return pl.pallas_call(
        _proj_kernel,
        grid=(rows // tm,),
        in_specs=[pl.BlockSpec((tm, d), row),
                  _const_spec((1, d)),
                  _const_spec(w["w_in"].shape),
                  _const_spec((1, MLA_Q_RANK)),
                  _const_spec((1, MLA_KV_RANK)),
                  _const_spec(w["w_qb"].shape),
                  _const_spec(w["w_uk"].shape)]
                 + [pl.BlockSpec((tm, LANES), tab)] * 6,
        out_specs=[pl.BlockSpec((tm, n), row) for n, _ in out_widths],
        out_shape=[jax.ShapeDtypeStruct((rows, n), dt) for n, dt in out_widths],
        compiler_params=pltpu.CompilerParams(
            dimension_semantics=("arbitrary",), vmem_limit_bytes=VMEM_LIMIT),
        name="proj",
    )(x, w["g_attn"], w["w_in"], w["g_q_a"], w["g_kv_a"], w["w_qb"], w["w_uk"], *tables)


def _t(x):
    return x.astype(F32).T


def _online_softmax_t(s, m_prev, l_prev, log2_scale):
    m_new = jnp.maximum(m_prev, jnp.max(s, axis=0, keepdims=True))
    alpha = jnp.exp2((m_prev - m_new) * log2_scale)
    p = jnp.exp2((s - m_new) * log2_scale)
    return p, alpha, m_new, alpha * l_prev + jnp.sum(p, axis=0, keepdims=True)


def _mla_attn_kernel(qabs_ref, qpe_ref, cb_ref, krb_ref, cmeta_ref, krmeta_ref, o_ref,
                     qt_ref, ct_ref, cm_ref, cmt_ref, m_ref, l_ref, acc_ref, s_ref, p_ref,
                     a_ref, *, bq):
    qi = pl.program_id(1)
    seq = cb_ref.shape[0]
    log2_scale = MLA_SCALE * math.log2(math.e)

    @pl.when(qi == 0)
    def _():
        for kb in range(seq // bq):
            ct_ref[kb] = _t(cb_ref[kb * bq:(kb + 1) * bq, :]).astype(BF16)
        cm_ref[...] = jnp.zeros(cm_ref.shape, BF16)
        cm_ref[0:N_META, :] = cmeta_ref[...]
        cmt_ref[...] = _t(cm_ref[...]).astype(BF16)

    for h in range(MLA_HEADS):
        qt_ref[h, 0:MLA_KV_RANK, :] = _t(
            qabs_ref[:, h * MLA_KV_RANK:(h + 1) * MLA_KV_RANK]).astype(BF16)
    per_chunk = LANES // MLA_ROPE
    for j in range(MLA_HEADS // per_chunk):
        t = _t(qpe_ref[:, j * LANES:(j + 1) * LANES]).astype(BF16)
        for i in range(per_chunk):
            qt_ref[j * per_chunk + i, MLA_KV_RANK:MLA_KV_RANK + MLA_ROPE, :] = (
                t[i * MLA_ROPE:(i + 1) * MLA_ROPE])
    for h in range(MLA_HEADS):
        s_ref[h, 0:N_META, :] = (
            _dot(cmeta_ref[...], qt_ref[h, 0:MLA_KV_RANK, :])
            + _dot(krmeta_ref[...], qt_ref[h, MLA_KV_RANK:MLA_KV_RANK + MLA_ROPE, :]))
    for h in range(MLA_HEADS):
        s = s_ref[h, 0:N_META, :]
        m0 = jnp.max(s, axis=0, keepdims=True)
        p = jnp.exp2((s - m0) * log2_scale)
        m_ref[h:h + 1, :] = m0
        l_ref[h:h + 1, :] = jnp.sum(p, axis=0, keepdims=True)
        p_ref[h, 0:N_META, :] = p.astype(BF16)
        p_ref[h, N_META:LANES, :] = jnp.zeros((LANES - N_META, bq), BF16)
    for h in range(MLA_HEADS):
        acc_ref[h] = _dot(cmt_ref[...], p_ref[h, 0:LANES, :])

    def attend(kc, kkr, ct, visible):
        n = kc.shape[0]
        for h in range(MLA_HEADS):
            s_ref[h, 0:n, :] = (_dot(kc, qt_ref[h, 0:MLA_KV_RANK, :])
                                + _dot(kkr, qt_ref[h, MLA_KV_RANK:MLA_KV_RANK + MLA_ROPE, :]))
        for h in range(MLA_HEADS):
            s = s_ref[h, 0:n, :]
            if visible is not None:
                s = jnp.where(visible, s, NEG_INF)
            p, alpha, m_new, l_new = _online_softmax_t(
                s, m_ref[h:h + 1, :], l_ref[h:h + 1, :], log2_scale)
            p_ref[h, 0:n, :] = p.astype(BF16)
            a_ref[h:h + 1, :] = alpha
            m_ref[h:h + 1, :] = m_new
            l_ref[h:h + 1, :] = l_new
        for h in range(MLA_HEADS):
            acc_ref[h] = a_ref[h:h + 1, :] * acc_ref[h] + _dot(ct, p_ref[h, 0:n, :])

    def body(kb, carry):
        start = pl.multiple_of(kb * bq, bq)
        attend(cb_ref[pl.ds(start, bq), :], krb_ref[pl.ds(start, bq), :], ct_ref[kb], None)
        return carry

    lax.fori_loop(0, qi, body, 0)
    start = pl.multiple_of(qi * bq, bq)
    key = lax.broadcasted_iota(jnp.int32, (bq, bq), 0)
    tok = lax.broadcasted_iota(jnp.int32, (bq, bq), 1)
    attend(cb_ref[pl.ds(start, bq), :], krb_ref[pl.ds(start, bq), :], ct_ref[qi], key <= tok)

    for h in range(MLA_HEADS):
        o_ref[:, h * MLA_KV_RANK:(h + 1) * MLA_KV_RANK] = (
            acc_ref[h] / l_ref[h:h + 1, :]).T.astype(o_ref.dtype)


def _mla_attn(qabs, qpe, cb, krb, cmeta, krmeta, batch, seq, bq):
    nq = seq // bq
    qrow = lambda b, i: (b * nq + i, 0)
    kv = lambda b, i: (b, 0)
    return pl.pallas_call(
        functools.partial(_mla_attn_kernel, bq=bq),
        grid=(batch, nq),
        in_specs=[pl.BlockSpec((bq, MLA_HEADS * MLA_KV_RANK), qrow),
                  pl.BlockSpec((bq, MLA_HEADS * MLA_ROPE), qrow),
                  pl.BlockSpec((seq, MLA_KV_RANK), kv),
                  pl.BlockSpec((seq, MLA_ROPE), kv),
                  _const_spec(cmeta.shape),
                  _const_spec(krmeta.shape)],
        out_specs=pl.BlockSpec((bq, MLA_HEADS * MLA_KV_RANK), qrow),
        out_shape=jax.ShapeDtypeStruct((batch * seq, MLA_HEADS * MLA_KV_RANK), BF16),
        scratch_shapes=[pltpu.VMEM((MLA_HEADS, MLA_KV_RANK + MLA_ROPE, bq), BF16),
                        pltpu.VMEM((nq, MLA_KV_RANK, bq), BF16),
                        pltpu.VMEM((LANES, MLA_KV_RANK), BF16),
                        pltpu.VMEM((MLA_KV_RANK, LANES), BF16),
                        pltpu.VMEM((MLA_HEADS, bq), F32),
                        pltpu.VMEM((MLA_HEADS, bq), F32),
                        pltpu.VMEM((MLA_HEADS, MLA_KV_RANK, bq), F32),
                        pltpu.VMEM((MLA_HEADS, bq, bq), F32),
                        pltpu.VMEM((MLA_HEADS, bq, bq), BF16),
                        pltpu.VMEM((MLA_HEADS, bq), F32)],
        compiler_params=pltpu.CompilerParams(
            dimension_semantics=("arbitrary", "arbitrary"), vmem_limit_bytes=VMEM_LIMIT),
        name="mla_attn",
    )(qabs, qpe, cb, krb, cmeta, krmeta)


def _lambda(lq1_ref, lk1_ref, lq2_ref, lk2_ref, lam_init):
    a = jnp.sum(lq1_ref[...] * lk1_ref[...], axis=-1, keepdims=True)
    b = jnp.sum(lq2_ref[...] * lk2_ref[...], axis=-1, keepdims=True)
    return jnp.exp(a) - jnp.exp(b) + lam_init


def _fill_diff_queries(qd_ref, dq_ref, n_tok):
    qd_ref[...] = jnp.zeros(qd_ref.shape, qd_ref.dtype)
    lane = lax.broadcasted_iota(jnp.int32, (n_tok, LANES), 1)
    for g in range(DIFF_KV_HEADS):
        for r in range(DIFF_REP):
            col = (g * DIFF_REP + r) * LANES
            x = dq_ref[:, col:col + LANES] * DIFF_SCALE
            for m in range(2):
                row = ((g * DIFF_REP + r) * 2 + m) * n_tok
                keep = (lane < DIFF_HEAD_DIM) if m == 0 else (lane >= DIFF_HEAD_DIM)
                qd_ref[row:row + n_tok, g * LANES:(g + 1) * LANES] = jnp.where(
                    keep, x, jnp.zeros_like(x)).astype(qd_ref.dtype)


def _finish_diff(o_ref, acc_ref, l_ref, lam, g_sub, lam_init, n_tok):
    for g in range(DIFF_KV_HEADS):
        for r in range(DIFF_REP):
            r0 = ((g * DIFF_REP + r) * 2) * n_tok
            r1 = r0 + n_tok
            o0 = acc_ref[r0:r0 + n_tok, :] / l_ref[r0:r0 + n_tok, :]
            o1 = acc_ref[r1:r1 + n_tok, :] / l_ref[r1:r1 + n_tok, :]
            d = _rms(o0 - lam * o1, g_sub, SUBLN_EPS) * (1.0 - lam_init)
            col = (g * DIFF_REP + r) * LANES
            o_ref[:, col:col + LANES] = d.astype(o_ref.dtype)


def _diff_attn_kernel(dq_ref, dkb_ref, dvb_ref, dkmeta_ref, dvmeta_ref,
                      lq1_ref, lk1_ref, lq2_ref, lk2_ref, g_sub_ref, o_ref,
                      qt_ref, vt_ref, vm_ref, vmt_ref, m_ref, l_ref, acc_ref,
                      s_ref, p_ref, a_ref, *, bq, bk, lam_init):
    qi = pl.program_id(1)
    seq = dkb_ref.shape[0]
    n_chunks = DIFF_KV_HEADS * DIFF_REP
    log2e = math.log2(math.e)

    @pl.when(qi == 0)
    def _():
        for kb in range(seq // bk):
            vt_ref[kb] = _t(dvb_ref[kb * bk:(kb + 1) * bk, :]).astype(BF16)
        vm_ref[...] = jnp.zeros(vm_ref.shape, BF16)
        vm_ref[0:N_META, :] = dvmeta_ref[...]
        vmt_ref[...] = _t(vm_ref[...]).astype(BF16)

    dim = lax.broadcasted_iota(jnp.int32, (LANES, bq), 0)
    for c in range(n_chunks):
        x = _t(dq_ref[:, c * LANES:(c + 1) * LANES] * DIFF_SCALE)
        qt_ref[c, :, 0:bq] = jnp.where(dim < DIFF_HEAD_DIM, x, 0.0).astype(BF16)
        qt_ref[c, :, bq:2 * bq] = jnp.where(dim >= DIFF_HEAD_DIM, x, 0.0).astype(BF16)

    for c in range(n_chunks):
        g = c // DIFF_REP
        s_ref[c, 0:N_META, :] = _dot(dkmeta_ref[:, g * LANES:(g + 1) * LANES], qt_ref[c])
    for c in range(n_chunks):
        s = s_ref[c, 0:N_META, :]
        m0 = jnp.max(s, axis=0, keepdims=True)
        p = jnp.exp2((s - m0) * log2e)
        m_ref[c:c + 1, :] = m0
        l_ref[c:c + 1, :] = jnp.sum(p, axis=0, keepdims=True)
        p_ref[c, 0:N_META, :] = p.astype(BF16)
        p_ref[c, N_META:LANES, :] = jnp.zeros((LANES - N_META, 2 * bq), BF16)
    for c in range(n_chunks):
        g = c // DIFF_REP
        acc_ref[c] = _dot(vmt_ref[g * LANES:(g + 1) * LANES, :], p_ref[c, 0:LANES, :])

    def attend(dk, vt, visible):
        n = dk.shape[0]
        for c in range(n_chunks):
            g = c // DIFF_REP
            s_ref[c, 0:n, :] = _dot(dk[:, g * LANES:(g + 1) * LANES], qt_ref[c])
        for c in range(n_chunks):
            s = s_ref[c, 0:n, :]
            if visible is not None:
                s = jnp.where(visible, s, NEG_INF)
            p, alpha, m_new, l_new = _online_softmax_t(
                s, m_ref[c:c + 1, :], l_ref[c:c + 1, :], log2e)
            p_ref[c, 0:n, :] = p.astype(BF16)
            a_ref[c:c + 1, :] = alpha
            m_ref[c:c + 1, :] = m_new
            l_ref[c:c + 1, :] = l_new
        for c in range(n_chunks):
            g = c // DIFF_REP
            acc_ref[c] = a_ref[c:c + 1, :] * acc_ref[c] + _dot(
                vt[g * LANES:(g + 1) * LANES, :], p_ref[c, 0:n, :])

    n_full = (qi * bq) // bk

    def body(kb, carry):
        start = pl.multiple_of(kb * bk, bk)
        attend(dkb_ref[pl.ds(start, bk), :], vt_ref[kb], None)
        return carry

    lax.fori_loop(0, n_full, body, 0)
    start = pl.multiple_of(n_full * bk, bk)
    key = start + lax.broadcasted_iota(jnp.int32, (bk, 2 * bq), 0)
    tok = qi * bq + (lax.broadcasted_iota(jnp.int32, (bk, 2 * bq), 1) & (bq - 1))
    attend(dkb_ref[pl.ds(start, bk), :], vt_ref[n_full], key <= tok)

    lam = _lambda(lq1_ref, lk1_ref, lq2_ref, lk2_ref, lam_init)
    for c in range(n_chunks):
        o = acc_ref[c] / l_ref[c:c + 1, :]
        d = (o[:, 0:bq] - lam * o[:, bq:2 * bq]).T
        d = _rms(d, g_sub_ref[...], SUBLN_EPS) * (1.0 - lam_init)
        o_ref[:, c * LANES:(c + 1) * LANES] = d.astype(o_ref.dtype)


def _diff_attn(dq, dkb, dvb, dkmeta, dvmeta, lams, g_sub, batch, seq, bq, bk, lam_init):
    assert bq & (bq - 1) == 0 and bk % bq == 0 and seq % bk == 0
    nq = seq // bq
    n_chunks = DIFF_KV_HEADS * DIFF_REP
    qrow = lambda b, i: (b * nq + i, 0)
    kv = lambda b, i: (b, 0)
    return pl.pallas_call(
        functools.partial(_diff_attn_kernel, bq=bq, bk=bk, lam_init=lam_init),
        grid=(batch, nq),
        in_specs=[pl.BlockSpec((bq, DIFF_WIDTH), qrow),
                  pl.BlockSpec((seq, DIFF_KV_WIDTH), kv),
                  pl.BlockSpec((seq, DIFF_KV_WIDTH), kv),
                  _const_spec(dkmeta.shape),
                  _const_spec(dvmeta.shape)]
                 + [_const_spec((1, DIFF_HEAD_DIM))] * 4
                 + [_const_spec((1, 2 * DIFF_HEAD_DIM))],
        out_specs=pl.BlockSpec((bq, DIFF_WIDTH), qrow),
        out_shape=jax.ShapeDtypeStruct((batch * seq, DIFF_WIDTH), BF16),
        scratch_shapes=[pltpu.VMEM((n_chunks, LANES, 2 * bq), BF16),
                        pltpu.VMEM((seq // bk, DIFF_KV_WIDTH, bk), BF16),
                        pltpu.VMEM((LANES, DIFF_KV_WIDTH), BF16),
                        pltpu.VMEM((DIFF_KV_WIDTH, LANES), BF16),
                        pltpu.VMEM((n_chunks, 2 * bq), F32),
                        pltpu.VMEM((n_chunks, 2 * bq), F32),
                        pltpu.VMEM((n_chunks, 2 * DIFF_HEAD_DIM, 2 * bq), F32),
                        pltpu.VMEM((n_chunks, bk, 2 * bq), F32),
                        pltpu.VMEM((n_chunks, bk, 2 * bq), BF16),
                        pltpu.VMEM((n_chunks, 2 * bq), F32)],
        compiler_params=pltpu.CompilerParams(
            dimension_semantics=("arbitrary", "arbitrary"), vmem_limit_bytes=VMEM_LIMIT),
        name="diff_attn",
    )(dq, dkb, dvb, dkmeta, dvmeta, *lams, g_sub)


def _paged_attn_kernel(pt_ref, qabs_ref, qpe_ref, dq_ref, c_new_ref, kr_new_ref, dk_new_ref,
                       dv_new_ref, lq1_ref, lk1_ref, lq2_ref, lk2_ref, g_sub_ref,
                       lat_hbm, krt_hbm, dkt_hbm, dv_hbm, mla_o_ref, diff_o_ref,
                       qm_ref, qp_ref, qd_ref, kc_ref, krt_ref, dkt_ref, dv_ref, new_c_ref,
                       new_kr_ref, new_dk_ref, new_dv_ref, mm_ref, lm_ref, accm_ref, md_ref,
                       ld_ref, accd_ref, lat_buf, krt_buf, dkt_buf, dv_buf, sem,
                       *, n_tok, pages, lam_init):
    b = pl.program_id(0)
    j = pl.program_id(1)
    n_steps = pl.num_programs(1)
    half = DIFF_ROWS // DIFF_KV_HEADS * n_tok

    t = b * n_steps + j
    n_total = pl.num_programs(0) * n_steps
    slot = lax.rem(t, PAGE_SLOTS)
    pools = ((lat_hbm, lat_buf), (krt_hbm, krt_buf), (dkt_hbm, dkt_buf), (dv_hbm, dv_buf))

    def page_copies(step, into):
        seq = step // n_steps
        first = (step - seq * n_steps) * pages
        return [pltpu.make_async_copy(src.at[0, pt_ref[seq, first + p_]],
                                      dst.at[into, p_], sem.at[into])
                for p_ in range(pages) for src, dst in pools]

    @pl.when(t == 0)
    def _():
        for ahead in range(PAGE_SLOTS - 1):
            @pl.when(ahead < n_total)
            def _():
                for copy in page_copies(ahead, ahead):
                    copy.start()

    @pl.when(t + PAGE_SLOTS - 1 < n_total)
    def _():
        for copy in page_copies(t + PAGE_SLOTS - 1, lax.rem(t + PAGE_SLOTS - 1, PAGE_SLOTS)):
            copy.start()

    for copy in page_copies(t, slot):
        copy.wait()
    page_refs = [buf.at[slot, p_] for p_ in range(pages) for _, buf in pools]

    def update(s_m, s_d, mix_m, mix_d):
        m_prev = mm_ref[...]
        m_new = jnp.maximum(m_prev, jnp.max(s_m, axis=-1, keepdims=True))
        alpha = jnp.exp(m_prev - m_new)
        p = jnp.exp(s_m - m_new)
        lm_ref[...] = alpha * lm_ref[...] + jnp.sum(p, axis=-1, keepdims=True)
        accm_ref[...] = alpha * accm_ref[...] + mix_m(p.astype(BF16))
        mm_ref[...] = m_new
        m_prev = md_ref[...]
        m_new = jnp.maximum(m_prev, jnp.max(s_d, axis=-1, keepdims=True))
        alpha = jnp.exp(m_prev - m_new)
        p = jnp.exp(s_d - m_new)
        ld_ref[...] = alpha * ld_ref[...] + jnp.sum(p, axis=-1, keepdims=True)
        p = p.astype(BF16)
        for g in range(DIFF_KV_HEADS):
            rows = slice(g * half, (g + 1) * half)
            accd_ref[rows, :] = alpha[rows] * accd_ref[rows, :] + mix_d(p[rows], g)
        md_ref[...] = m_new

    @pl.when(j == 0)
    def _():
        for h in range(MLA_HEADS):
            qm_ref[h * n_tok:(h + 1) * n_tok, :] = qabs_ref[:, h * MLA_KV_RANK:(h + 1) * MLA_KV_RANK]
            qp_ref[h * n_tok:(h + 1) * n_tok, :] = qpe_ref[:, h * MLA_ROPE:(h + 1) * MLA_ROPE]
        _fill_diff_queries(qd_ref, dq_ref, n_tok)
        mm_ref[...] = jnp.full(mm_ref.shape, NEG_INF, F32)
        lm_ref[...] = jnp.zeros(lm_ref.shape, F32)
        accm_ref[...] = jnp.zeros(accm_ref.shape, F32)
        md_ref[...] = jnp.full(md_ref.shape, NEG_INF, F32)
        ld_ref[...] = jnp.zeros(ld_ref.shape, F32)
        accd_ref[...] = jnp.zeros(accd_ref.shape, F32)
        for dst, src in ((new_c_ref, c_new_ref), (new_kr_ref, kr_new_ref),
                         (new_dk_ref, dk_new_ref), (new_dv_ref, dv_new_ref)):
            dst[...] = jnp.zeros(dst.shape, F32)
            dst[0:n_tok, :] = src[...]
        new_c = new_c_ref[...].astype(BF16)
        new_dv = new_dv_ref[...].astype(BF16)
        s_m = (_dot_nt(qm_ref[...].astype(BF16), new_c)
               + _dot_nt(qp_ref[...].astype(BF16), new_kr_ref[...].astype(BF16))) * MLA_SCALE
        s_d = _dot_nt(qd_ref[...].astype(BF16), new_dk_ref[...].astype(BF16))
        update(_causal(s_m, n_tok, 0, 0), _causal(s_d, n_tok, 0, 0),
               lambda p: _dot(p, new_c),
               lambda p, g: _dot(p, new_dv[:, g * LANES:(g + 1) * LANES]))

    for p_ in range(pages):
        span = slice(p_ * PAGE_SIZE, (p_ + 1) * PAGE_SIZE)
        kc_ref[span, :] = page_refs[4 * p_][...].astype(BF16)
        krt_ref[:, span] = page_refs[4 * p_ + 1][...].astype(BF16)
        dkt_ref[:, span] = page_refs[4 * p_ + 2][...].astype(BF16)
        for g in range(DIFF_KV_HEADS):
            dv_ref[g, span, :] = page_refs[4 * p_ + 3][
                pl.ds(g, PAGE_SIZE, stride=DIFF_KV_HEADS), :].astype(BF16)
    s_m = (_dot_nt(qm_ref[...].astype(BF16), kc_ref[...])
           + _dot(qp_ref[...].astype(BF16), krt_ref[...])) * MLA_SCALE
    s_d = _dot(qd_ref[...].astype(BF16), dkt_ref[...])
    update(s_m, s_d, lambda p: _dot(p, kc_ref[...]), lambda p, g: _dot(p, dv_ref[g]))

    @pl.when(j == pl.num_programs(1) - 1)
    def _():
        for h in range(MLA_HEADS):
            rows = slice(h * n_tok, (h + 1) * n_tok)
            mla_o_ref[:, h * MLA_KV_RANK:(h + 1) * MLA_KV_RANK] = accm_ref[rows, :] / lm_ref[rows, :]
        lam = _lambda(lq1_ref, lk1_ref, lq2_ref, lk2_ref, lam_init)
        _finish_diff(diff_o_ref, accd_ref, ld_ref, lam, g_sub_ref[...], lam_init, n_tok)


def _paged_attn(page_table, qabs, qpe, dq, c_new, kr_new, dk_new, dv_new, lams, g_sub,
                pools, n_tok, pages, lam_init):
    n_seq, n_pages = page_table.shape
    seq_row = lambda b, j, pt: (b, 0)
    const = lambda b, j, pt: (0, 0)
    page_specs = [pl.BlockSpec(memory_space=pl.ANY)] * len(pools)
    page_bufs = [pltpu.VMEM((PAGE_SLOTS, pages) + pool.shape[2:], pool.dtype) for pool in pools]
    n_keys = pages * PAGE_SIZE
    mla_rows = MLA_HEADS * n_tok
    diff_rows = DIFF_ROWS * n_tok
    grid_spec = pltpu.PrefetchScalarGridSpec(
        num_scalar_prefetch=1,
        grid=(n_seq, n_pages // pages),
        in_specs=[pl.BlockSpec((n_tok, MLA_HEADS * MLA_KV_RANK), seq_row),
                  pl.BlockSpec((n_tok, MLA_HEADS * MLA_ROPE), seq_row),
                  pl.BlockSpec((n_tok, DIFF_WIDTH), seq_row),
                  pl.BlockSpec((n_tok, MLA_KV_RANK), seq_row),
                  pl.BlockSpec((n_tok, MLA_ROPE), seq_row),
                  pl.BlockSpec((n_tok, DIFF_KV_WIDTH), seq_row),
                  pl.BlockSpec((n_tok, DIFF_KV_WIDTH), seq_row)]
                 + [pl.BlockSpec((1, DIFF_HEAD_DIM), const)] * 4
                 + [pl.BlockSpec((1, 2 * DIFF_HEAD_DIM), const)]
                 + page_specs,
        out_specs=[pl.BlockSpec((n_tok, MLA_HEADS * MLA_KV_RANK), seq_row),
                   pl.BlockSpec((n_tok, DIFF_WIDTH), seq_row)],
        scratch_shapes=[pltpu.VMEM((mla_rows, MLA_KV_RANK), F32),
                        pltpu.VMEM((mla_rows, MLA_ROPE), F32),
                        pltpu.VMEM((diff_rows, DIFF_KV_WIDTH), F32),
                        pltpu.VMEM((n_keys, MLA_KV_RANK), BF16),
                        pltpu.VMEM((MLA_ROPE, n_keys), BF16),
                        pltpu.VMEM((DIFF_KV_WIDTH, n_keys), BF16),
                        pltpu.VMEM((DIFF_KV_HEADS, n_keys, 2 * DIFF_HEAD_DIM), BF16),
                        pltpu.VMEM((PAGE_SIZE, MLA_KV_RANK), F32),
                        pltpu.VMEM((PAGE_SIZE, MLA_ROPE), F32),
                        pltpu.VMEM((PAGE_SIZE, DIFF_KV_WIDTH), F32),
                        pltpu.VMEM((PAGE_SIZE, DIFF_KV_WIDTH), F32),
                        pltpu.VMEM((mla_rows, 1), F32),
                        pltpu.VMEM((mla_rows, 1), F32),
                        pltpu.VMEM((mla_rows, MLA_KV_RANK), F32),
                        pltpu.VMEM((diff_rows, 1), F32),
                        pltpu.VMEM((diff_rows, 1), F32),
                        pltpu.VMEM((diff_rows, 2 * DIFF_HEAD_DIM), F32)]
                       + page_bufs + [pltpu.SemaphoreType.DMA((PAGE_SLOTS,))],
    )
    rows = n_seq * n_tok
    return pl.pallas_call(
        functools.partial(_paged_attn_kernel, n_tok=n_tok, pages=pages, lam_init=lam_init),
        grid_spec=grid_spec,
        out_shape=[jax.ShapeDtypeStruct((rows, MLA_HEADS * MLA_KV_RANK), F32),
                   jax.ShapeDtypeStruct((rows, DIFF_WIDTH), F32)],
        compiler_params=pltpu.CompilerParams(
            dimension_semantics=("arbitrary", "arbitrary"), vmem_limit_bytes=VMEM_LIMIT),
        name="paged_attn",
    )(page_table, qabs, qpe, dq, c_new, kr_new, dk_new, dv_new, *lams, g_sub, *pools)


def _out_proj_kernel(x_ref, lat_ref, diff_ref, w_uv_ref, g_mla_ref, w_out_ref, h_ref, mla_ref):
    for h in range(MLA_HEADS):
        lat = lat_ref[:, h * MLA_KV_RANK:(h + 1) * MLA_KV_RANK].astype(BF16)
        mla_ref[:, h * MLA_V:(h + 1) * MLA_V] = _dot(lat, w_uv_ref[h])
    mla = _rms(mla_ref[...], g_mla_ref[...], NORM_EPS).astype(BF16)
    h_ref[...] = (x_ref[...] + _dot(mla, w_out_ref[0:MLA_WIDTH, :])
                  + _dot(diff_ref[...].astype(BF16), w_out_ref[MLA_WIDTH:MLA_WIDTH + DIFF_WIDTH, :]))


def _out_proj(x, lat, diff, w, tm):
    rows, d = x.shape
    row = lambda i: (i, 0)
    return pl.pallas_call(
        _out_proj_kernel,
        grid=(rows // tm,),
        in_specs=[pl.BlockSpec((tm, d), row),
                  pl.BlockSpec((tm, lat.shape[1]), row),
                  pl.BlockSpec((tm, diff.shape[1]), row),
                  _const_spec(w["w_uv"].shape),
                  _const_spec((1, MLA_WIDTH)),
                  _const_spec(w["w_out"].shape)],
        out_specs=pl.BlockSpec((tm, d), row),
        out_shape=jax.ShapeDtypeStruct((rows, d), F32),
        scratch_shapes=[pltpu.VMEM((tm, MLA_WIDTH), F32)],
        compiler_params=pltpu.CompilerParams(
            dimension_semantics=("arbitrary",), vmem_limit_bytes=VMEM_LIMIT),
        name="out_proj",
    )(x, lat, diff, w["w_uv"], w["g_mla_out"], w["w_out"])


def _ffn_kernel(h_ref, g_ffn_ref, w_gate_ref, w_up_ref, w_down_ref, g_final_ref, o_ref, xn_ref):
    j = pl.program_id(1)

    @pl.when(j == 0)
    def _():
        h = h_ref[...]
        xn_ref[...] = _rms(h, g_ffn_ref[...], NORM_EPS).astype(BF16)
        o_ref[...] = h

    xn = xn_ref[...]
    gate = _dot(xn, w_gate_ref[...])
    up = _dot(xn, w_up_ref[...])
    act = (gate * (1.0 / (1.0 + jnp.exp(-gate))) * up).astype(BF16)
    o_ref[...] += _dot(act, w_down_ref[...])

    @pl.when(j == pl.num_programs(1) - 1)
    def _():
        o_ref[...] = _rms(o_ref[...], g_final_ref[...], NORM_EPS)


def _ffn(h, w, tm, th):
    rows, d = h.shape
    hidden = w["w_gate"].shape[1]
    row = lambda i, j: (i, 0)
    return pl.pallas_call(
        _ffn_kernel,
        grid=(rows // tm, hidden // th),
        in_specs=[pl.BlockSpec((tm, d), row),
                  pl.BlockSpec((1, d), lambda i, j: (0, 0)),
                  pl.BlockSpec((d, th), lambda i, j: (0, j)),
                  pl.BlockSpec((d, th), lambda i, j: (0, j)),
                  pl.BlockSpec((th, d), lambda i, j: (j, 0)),
                  pl.BlockSpec((1, d), lambda i, j: (0, 0))],
        out_specs=pl.BlockSpec((tm, d), row),
        out_shape=jax.ShapeDtypeStruct((rows, d), F32),
        scratch_shapes=[pltpu.VMEM((tm, d), BF16)],
        compiler_params=pltpu.CompilerParams(
            dimension_semantics=("arbitrary", "arbitrary"), vmem_limit_bytes=VMEM_LIMIT),
        name="ffn",
    )(h, w["g_ffn"], w["w_gate"], w["w_up"], w["w_down"], w["g_final"])


def _rope_tables(pos, rot_dim):
    half = rot_dim // 2
    inv = ROPE_THETA ** (-jnp.arange(half, dtype=F32) * 2.0 / rot_dim)
    ang = pos.astype(F32)[:, None] * inv
    d = jnp.arange(LANES) % DIFF_HEAD_DIM
    a = ang[:, d % half]
    cos = jnp.where(d < rot_dim, jnp.cos(a), 1.0)
    sin = jnp.sin(a)
    sin_lo = jnp.where(d < half, -sin, 0.0)
    sin_hi = jnp.where((d >= half) & (d < rot_dim), sin, 0.0)
    return cos.astype(F32), sin_lo.astype(F32), sin_hi.astype(F32)


def _tables(pos):
    return _rope_tables(pos, MLA_ROPE) + _rope_tables(pos, DIFF_ROT)


def _row_block(rows, target):
    tm = min(rows, target)
    while rows % tm:
        tm //= 2
    return tm


def kernel(x_prompt, x_sample, cache_mla_latent, cache_mla_krope, cache_diff_k, cache_diff_v,
           page_table, meta_tokens, g_attn, w_in, g_q_a, w_q_b, g_kv_a, w_uk, w_uv, g_mla_out,
           lambda_q1, lambda_k1, lambda_q2, lambda_k2, g_subln, w_out, g_ffn, w_gate, w_up,
           w_down, g_final):
    batch, seq, d = x_prompt.shape
    n_seq, n_tok, _ = x_sample.shape
    depth = w_in.shape[0]
    assert depth == 1, "single-layer trunk"
    n_phys = cache_mla_latent.shape[1]
    past_len = page_table.shape[1] * PAGE_SIZE
    lam_init = 0.8 - 0.6 * math.exp(-0.3 * 0)

    wi = w_in[0]
    o = [0, MLA_Q_RANK, MLA_Q_RANK + MLA_KV_RANK, MLA_Q_RANK + MLA_KV_RANK + MLA_ROPE]
    o.append(o[3] + DIFF_WIDTH)
    o.append(o[4] + DIFF_KV_WIDTH)
    o.append(o[5] + DIFF_KV_WIDTH)
    w_in_r = jnp.concatenate(
        [wi[:, o[0]:o[2]], wi[:, o[3]:o[6]], wi[:, o[2]:o[3]],
         jnp.zeros((d, LANES - MLA_ROPE), wi.dtype)], axis=1).astype(BF16)
    wq = w_q_b[0]
    w_qb = jnp.concatenate(
        [wq[:, :, :MLA_QK_NOPE].reshape(MLA_Q_RANK, -1),
         wq[:, :, MLA_QK_NOPE:].reshape(MLA_Q_RANK, -1)], axis=1).astype(BF16)
    w = {
        "g_attn": g_attn[0][None], "w_in": w_in_r, "g_q_a": g_q_a[0][None],
        "g_kv_a": g_kv_a[0][None], "w_qb": w_qb,
        "w_uk": jnp.transpose(w_uk[0], (1, 2, 0)).astype(BF16),
        "w_uv": jnp.transpose(w_uv[0], (1, 0, 2)).astype(BF16),
        "g_mla_out": g_mla_out[0][None], "w_out": w_out[0].astype(BF16),
        "g_ffn": g_ffn[0][None], "w_gate": w_gate[0].astype(BF16),
        "w_up": w_up[0].astype(BF16), "w_down": w_down[0].astype(BF16),
        "g_final": g_final[None],
    }
    lams = (lambda_q1, lambda_k1, lambda_q2, lambda_k2)
    g_sub = g_subln[0][None]

    xp = x_prompt.reshape(batch * seq, d)
    xs = x_sample.reshape(n_seq * n_tok, d)
    tm_p = _row_block(seq, 256)
    tm_s = _row_block(n_seq * n_tok, 256)
    tab_p = _tables(N_META + jnp.arange(seq))
    tab_s = _tables(jnp.tile(past_len + jnp.arange(n_tok), tm_s // n_tok))
    tab_m = _tables(jnp.arange(N_META))
    (qabs_p, qpe_p, dq_p, c_p, kr_p, dk_p, dv_p, cb_p, krb_p, dkb_p, dvb_p) = _proj(
        xp, tab_p, tm_p, BF16, w)
    (qabs_s, qpe_s, dq_s, c_s, kr_s, dk_s, dv_s, _, _, _, _) = _proj(xs, tab_s, tm_s, F32, w)
    (_, _, _, c_m, kr_m, dk_m, dv_m, cb_m, krb_m, dkb_m, dvb_m) = _proj(
        meta_tokens.astype(F32), tab_m, N_META, BF16, w)

    lat_p = _mla_attn(qabs_p, qpe_p, cb_p, krb_p, cb_m, krb_m, batch, seq, _row_block(seq, 256))
    diff_p = _diff_attn(dq_p, dkb_p, dvb_p, dkb_m, dvb_m, lams, g_sub, batch, seq,
                        _row_block(seq, 128), _row_block(seq, 256), lam_init)

    pools = (cache_mla_latent,
             jnp.swapaxes(cache_mla_krope, 2, 3),
             jnp.transpose(cache_diff_k, (0, 1, 3, 4, 5, 2)).reshape(
                 depth, n_phys, DIFF_KV_WIDTH, PAGE_SIZE),
             cache_diff_v.reshape(depth, n_phys, PAGE_SIZE * DIFF_KV_HEADS, 2 * DIFF_HEAD_DIM))
    pages = _row_block(page_table.shape[1], 16)
    lat_s, diff_s = _paged_attn(page_table, qabs_s, qpe_s, dq_s, c_s, kr_s, dk_s, dv_s, lams,
                                g_sub, pools, n_tok, pages, lam_init)

    h_p = _out_proj(xp, lat_p, diff_p, w, _row_block(batch * seq, 512))
    h_s = _out_proj(xs, lat_s, diff_s, w, _row_block(n_seq * n_tok, 512))
    hidden = w_gate.shape[2]
    th = 512 if hidden % 512 == 0 else hidden
    y_p = _ffn(h_p, w, _row_block(batch * seq, 512), th).reshape(batch, seq, d)
    y_s = _ffn(h_s, w, _row_block(n_seq * n_tok, 512), th).reshape(n_seq, n_tok, d)

    def with_meta(meta, real):
        n = real.shape[-1]
        full = jnp.concatenate(
            [jnp.broadcast_to(meta[None], (batch, N_META, n)), real.reshape(batch, seq, n)], axis=1)
        return full[None]

    t = seq + N_META
    kv_shape = (DIFF_KV_HEADS, 2, DIFF_HEAD_DIM)
    v_shape = (DIFF_KV_HEADS, 2 * DIFF_HEAD_DIM)
    return (y_p, y_s,
            with_meta(c_m, c_p), with_meta(kr_m, kr_p),
            with_meta(dk_m, dk_p).reshape((1, batch, t) + kv_shape),
            with_meta(dv_m, dv_p).reshape((1, batch, t) + v_shape),
            c_s.reshape(1, n_seq, n_tok, -1), kr_s.reshape(1, n_seq, n_tok, -1),
            dk_s.reshape((1, n_seq, n_tok) + kv_shape),
            dv_s.reshape((1, n_seq, n_tok) + v_shape))
```

```python
import functools
import math

import jax
import jax.numpy as jnp
from jax import lax
from jax.experimental import pallas as pl
from jax.experimental.pallas import tpu as pltpu

F32 = jnp.float32
BF16 = jnp.bfloat16

N_META = 16
ROPE_THETA = 500000.0
NORM_EPS = 1e-6
SUBLN_EPS = 1e-5
NEG_INF = -1e30
PAGE_SIZE = 128

MLA_HEADS = 8
MLA_QK_NOPE = 128
MLA_ROPE = 64
MLA_V = 128
MLA_Q_RANK = 512
MLA_KV_RANK = 256
MLA_WIDTH = MLA_HEADS * MLA_V
MLA_SCALE = (MLA_QK_NOPE + MLA_ROPE) ** -0.5

DIFF_HEADS = 8
DIFF_KV_HEADS = 2
DIFF_REP = DIFF_HEADS // DIFF_KV_HEADS
DIFF_HEAD_DIM = 64
DIFF_ROT = DIFF_HEAD_DIM // 4
DIFF_WIDTH = DIFF_HEADS * 2 * DIFF_HEAD_DIM
DIFF_KV_WIDTH = DIFF_KV_HEADS * 2 * DIFF_HEAD_DIM
DIFF_SCALE = DIFF_HEAD_DIM ** -0.5
DIFF_ROWS = DIFF_KV_HEADS * DIFF_REP * 2

PAGE_SLOTS = 4
LANES = 128
VMEM_LIMIT = 56 * 1024 * 1024

OFF_QA = 0
OFF_KVA = OFF_QA + MLA_Q_RANK
OFF_DQ = OFF_KVA + MLA_KV_RANK
OFF_DK = OFF_DQ + DIFF_WIDTH
OFF_DV = OFF_DK + DIFF_KV_WIDTH
OFF_KR = OFF_DV + DIFF_KV_WIDTH
IN_WIDTH_PADDED = OFF_KR + LANES


def _rms(x, g, eps):
    return x * lax.rsqrt(jnp.mean(x * x, axis=-1, keepdims=True) + eps) * g


def _dot(a, b):
    return jnp.dot(a, b, preferred_element_type=F32)


def _dot_nt(a, b):
    return lax.dot_general(a, b, (((1,), (1,)), ((), ())), preferred_element_type=F32)


def _rope_chunk(x, cos, sin_lo, sin_hi, half):
    return (x * cos + pltpu.roll(x, LANES - half, 1) * sin_lo
            + pltpu.roll(x, half, 1) * sin_hi)


def _causal(s, n_tok, tok0, col0):
    assert n_tok & (n_tok - 1) == 0
    tok = tok0 + (lax.broadcasted_iota(jnp.int32, s.shape, 0) & (n_tok - 1))
    col = col0 + lax.broadcasted_iota(jnp.int32, s.shape, 1)
    return jnp.where(col <= tok, s, NEG_INF)


def _const_spec(shape):
    return pl.BlockSpec(shape, lambda *_: (0,) * len(shape))


def _proj_kernel(x_ref, g_attn_ref, w_in_ref, g_q_ref, g_kv_ref, w_qb_ref, w_uk_ref,
                 cos_m_ref, slo_m_ref, shi_m_ref, cos_d_ref, slo_d_ref, shi_d_ref,
                 qabs_ref, qpe_ref, dq_ref, c_ref, kr_ref, dk_ref, dv_ref,
                 cb_ref, krb_ref, dkb_ref, dvb_ref):
    xn = _rms(x_ref[...], g_attn_ref[...], NORM_EPS).astype(BF16)
    z = _dot(xn, w_in_ref[...])

    cos_m, slo_m, shi_m = cos_m_ref[...], slo_m_ref[...], shi_m_ref[...]
    cos_d, slo_d, shi_d = cos_d_ref[...], slo_d_ref[...], shi_d_ref[...]

    c = _rms(z[:, OFF_KVA:OFF_KVA + MLA_KV_RANK], g_kv_ref[...], NORM_EPS)
    c_ref[...] = c
    cb_ref[...] = c.astype(BF16)
    kr = _rope_chunk(z[:, OFF_KR:OFF_KR + LANES], cos_m, slo_m, shi_m, MLA_ROPE // 2)
    kr_ref[...] = kr[:, :MLA_ROPE]
    krb_ref[...] = kr[:, :MLA_ROPE].astype(BF16)

    qn = _rms(z[:, OFF_QA:OFF_QA + MLA_Q_RANK], g_q_ref[...], NORM_EPS).astype(BF16)
    q = _dot(qn, w_qb_ref[...])
    nope_w = MLA_HEADS * MLA_QK_NOPE
    for h in range(MLA_HEADS):
        qh = q[:, h * MLA_QK_NOPE:(h + 1) * MLA_QK_NOPE].astype(BF16)
        qabs_ref[:, h * MLA_KV_RANK:(h + 1) * MLA_KV_RANK] = _dot(qh, w_uk_ref[h]).astype(qabs_ref.dtype)
    for j in range(MLA_HEADS * MLA_ROPE // LANES):
        qc = q[:, nope_w + j * LANES:nope_w + (j + 1) * LANES]
        qpe_ref[:, j * LANES:(j + 1) * LANES] = _rope_chunk(
            qc, cos_m, slo_m, shi_m, MLA_ROPE // 2).astype(qpe_ref.dtype)

    for j in range(DIFF_WIDTH // LANES):
        xc = z[:, OFF_DQ + j * LANES:OFF_DQ + (j + 1) * LANES]
        dq_ref[:, j * LANES:(j + 1) * LANES] = _rope_chunk(
            xc, cos_d, slo_d, shi_d, DIFF_ROT // 2).astype(dq_ref.dtype)
    for j in range(DIFF_KV_WIDTH // LANES):
        xc = z[:, OFF_DK + j * LANES:OFF_DK + (j + 1) * LANES]
        dk = _rope_chunk(xc, cos_d, slo_d, shi_d, DIFF_ROT // 2)
        dk_ref[:, j * LANES:(j + 1) * LANES] = dk
        dkb_ref[:, j * LANES:(j + 1) * LANES] = dk.astype(BF16)
    dv = z[:, OFF_DV:OFF_DV + DIFF_KV_WIDTH]
    dv_ref[...] = dv
    dvb_ref[...] = dv.astype(BF16)


def _proj(x, tables, tm, q_dtype, w):
    rows, d = x.shape
    t_tab = tables[0].shape[0]
    n_tab = t_tab // tm
    row = lambda i: (i, 0)
    tab = lambda i: (i % n_tab, 0)
    out_widths = [(MLA_HEADS * MLA_KV_RANK, q_dtype), (MLA_HEADS * MLA_ROPE, q_dtype),
                  (DIFF_WIDTH, q_dtype), (MLA_KV_RANK, F32), (MLA_ROPE, F32),
                  (DIFF_KV_WIDTH, F32), (DIFF_KV_WIDTH, F32), (MLA_KV_RANK, BF16),
                  (MLA_ROPE, BF16), (DIFF_KV_WIDTH, BF16), (DIFF_KV_WIDTH, BF16)]
    return pl.pallas_call(
        _proj_kernel,
        grid=(rows // tm,),
        in_specs=[pl.BlockSpec((tm, d), row),
                  _const_spec((1, d)),
                  _const_spec(w["w_in"].shape),
                  _const_spec((1, MLA_Q_RANK)),
                  _const_spec((1, MLA_KV_RANK)),
                  _const_spec(w["w_qb"].shape),
                  _const_spec(w["w_uk"].shape)]
                 + [pl.BlockSpec((tm, LANES), tab)] * 6,
        out_specs=[pl.BlockSpec((tm, n), row) for n, _ in out_widths],
        out_shape=[jax.ShapeDtypeStruct((rows, n), dt) for n, dt in out_widths],
        compiler_params=pltpu.CompilerParams(
            dimension_semantics=("arbitrary",), vmem_limit_bytes=VMEM_LIMIT),
        name="proj",
    )(x, w["g_attn"], w["w_in"], w["g_q_a"], w["g_kv_a"], w["w_qb"], w["w_uk"], *tables)


def _t(x):
    return x.astype(F32).T


def _online_softmax_t(s, m_prev, l_prev, log2_scale):
    m_new = jnp.maximum(m_prev, jnp.max(s, axis=0, keepdims=True))
    alpha = jnp.exp2((m_prev - m_new) * log2_scale)
    p = jnp.exp2((s - m_new) * log2_scale)
    return p, alpha, m_new, alpha * l_prev + jnp.sum(p, axis=0, keepdims=True)


def _mla_attn_kernel(qabs_ref, qpe_ref, cb_ref, krb_ref, cmeta_ref, krmeta_ref, o_ref,
                     qt_ref, ct_ref, cm_ref, cmt_ref, m_ref, l_ref, acc_ref, s_ref, p_ref,
                     a_ref, *, bq):
    qi = pl.program_id(1)
    seq = cb_ref.shape[0]
    log2_scale = MLA_SCALE * math.log2(math.e)

    @pl.when(qi == 0)
    def _():
        for kb in range(seq // bq):
            ct_ref[kb] = _t(cb_ref[kb * bq:(kb + 1) * bq, :]).astype(BF16)
        cm_ref[...] = jnp.zeros(cm_ref.shape, BF16)
        cm_ref[0:N_META, :] = cmeta_ref[...]
        cmt_ref[...] = _t(cm_ref[...]).astype(BF16)

    for h in range(MLA_HEADS):
        qt_ref[h, 0:MLA_KV_RANK, :] = _t(
            qabs_ref[:, h * MLA_KV_RANK:(h + 1) * MLA_KV_RANK]).astype(BF16)
    per_chunk = LANES // MLA_ROPE
    for j in range(MLA_HEADS // per_chunk):
        t = _t(qpe_ref[:, j * LANES:(j + 1) * LANES]).astype(BF16)
        for i in range(per_chunk):
            qt_ref[j * per_chunk + i, MLA_KV_RANK:MLA_KV_RANK + MLA_ROPE, :] = (
                t[i * MLA_ROPE:(i + 1) * MLA_ROPE])
    for h in range(MLA_HEADS):
        s_ref[h, 0:N_META, :] = (
            _dot(cmeta_ref[...], qt_ref[h, 0:MLA_KV_RANK, :])
            + _dot(krmeta_ref[...], qt_ref[h, MLA_KV_RANK:MLA_KV_RANK + MLA_ROPE, :]))
    for h in range(MLA_HEADS):
        s = s_ref[h, 0:N_META, :]
        m0 = jnp.max(s, axis=0, keepdims=True)
        p = jnp.exp2((s - m0) * log2_scale)
        m_ref[h:h + 1, :] = m0
        l_ref[h:h + 1, :] = jnp.sum(p, axis=0, keepdims=True)
        p_ref[h, 0:N_META, :] = p.astype(BF16)
        p_ref[h, N_META:LANES, :] = jnp.zeros((LANES - N_META, bq), BF16)
    for h in range(MLA_HEADS):
        acc_ref[h] = _dot(cmt_ref[...], p_ref[h, 0:LANES, :])

    def attend(kc, kkr, ct, visible):
        n = kc.shape[0]
        for h in range(MLA_HEADS):
            s_ref[h, 0:n, :] = (_dot(kc, qt_ref[h, 0:MLA_KV_RANK, :])
                                + _dot(kkr, qt_ref[h, MLA_KV_RANK:MLA_KV_RANK + MLA_ROPE, :]))
        for h in range(MLA_HEADS):
            s = s_ref[h, 0:n, :]
            if visible is not None:
                s = jnp.where(visible, s, NEG_INF)
            p, alpha, m_new, l_new = _online_softmax_t(
                s, m_ref[h:h + 1, :], l_ref[h:h + 1, :], log2_scale)
            p_ref[h, 0:n, :] = p.astype(BF16)
            a_ref[h:h + 1, :] = alpha
            m_ref[h:h + 1, :] = m_new
            l_ref[h:h + 1, :] = l_new
        for h in range(MLA_HEADS):
            acc_ref[h] = a_ref[h:h + 1, :] * acc_ref[h] + _dot(ct, p_ref[h, 0:n, :])

    def body(kb, carry):
        start = pl.multiple_of(kb * bq, bq)
        attend(cb_ref[pl.ds(start, bq), :], krb_ref[pl.ds(start, bq), :], ct_ref[kb], None)
        return carry

    lax.fori_loop(0, qi, body, 0)
    start = pl.multiple_of(qi * bq, bq)
    key = lax.broadcasted_iota(jnp.int32, (bq, bq), 0)
    tok = lax.broadcasted_iota(jnp.int32, (bq, bq), 1)
    attend(cb_ref[pl.ds(start, bq), :], krb_ref[pl.ds(start, bq), :], ct_ref[qi], key <= tok)

    for h in range(MLA_HEADS):
        o_ref[:, h * MLA_KV_RANK:(h + 1) * MLA_KV_RANK] = (
            acc_ref[h] / l_ref[h:h + 1, :]).T.astype(o_ref.dtype)


def _mla_attn(qabs, qpe, cb, krb, cmeta, krmeta, batch, seq, bq):
    nq = seq // bq
    qrow = lambda b, i: (b * nq + i, 0)
    kv = lambda b, i: (b, 0)
    return pl.pallas_call(
        functools.partial(_mla_attn_kernel, bq=bq),
        grid=(batch, nq),
        in_specs=[pl.BlockSpec((bq, MLA_HEADS * MLA_KV_RANK), qrow),
                  pl.BlockSpec((bq, MLA_HEADS * MLA_ROPE), qrow),
                  pl.BlockSpec((seq, MLA_KV_RANK), kv),
                  pl.BlockSpec((seq, MLA_ROPE), kv),
                  _const_spec(cmeta.shape),
                  _const_spec(krmeta.shape)],
        out_specs=pl.BlockSpec((bq, MLA_HEADS * MLA_KV_RANK), qrow),
        out_shape=jax.ShapeDtypeStruct((batch * seq, MLA_HEADS * MLA_KV_RANK), BF16),
        scratch_shapes=[pltpu.VMEM((MLA_HEADS, MLA_KV_RANK + MLA_ROPE, bq), BF16),
                        pltpu.VMEM((nq, MLA_KV_RANK, bq), BF16),
                        pltpu.VMEM((LANES, MLA_KV_RANK), BF16),
                        pltpu.VMEM((MLA_KV_RANK, LANES), BF16),
                        pltpu.VMEM((MLA_HEADS, bq), F32),
                        pltpu.VMEM((MLA_HEADS, bq), F32),
                        pltpu.VMEM((MLA_HEADS, MLA_KV_RANK, bq), F32),
                        pltpu.VMEM((MLA_HEADS, bq, bq), F32),
                        pltpu.VMEM((MLA_HEADS, bq, bq), BF16),
                        pltpu.VMEM((MLA_HEADS, bq), F32)],
        compiler_params=pltpu.CompilerParams(
            dimension_semantics=("arbitrary", "arbitrary"), vmem_limit_bytes=VMEM_LIMIT),
        name="mla_attn",
    )(qabs, qpe, cb, krb, cmeta, krmeta)


def _lambda(lq1_ref, lk1_ref, lq2_ref, lk2_ref, lam_init):
    a = jnp.sum(lq1_ref[...] * lk1_ref[...], axis=-1, keepdims=True)
    b = jnp.sum(lq2_ref[...] * lk2_ref[...], axis=-1, keepdims=True)
    return jnp.exp(a) - jnp.exp(b) + lam_init


def _fill_diff_queries(qd_ref, dq_ref, n_tok):
    qd_ref[...] = jnp.zeros(qd_ref.shape, qd_ref.dtype)
    lane = lax.broadcasted_iota(jnp.int32, (n_tok, LANES), 1)
    for g in range(DIFF_KV_HEADS):
        for r in range(DIFF_REP):
            col = (g * DIFF_REP + r) * LANES
            x = dq_ref[:, col:col + LANES] * DIFF_SCALE
            for m in range(2):
                row = ((g * DIFF_REP + r) * 2 + m) * n_tok
                keep = (lane < DIFF_HEAD_DIM) if m == 0 else (lane >= DIFF_HEAD_DIM)
                qd_ref[row:row + n_tok, g * LANES:(g + 1) * LANES] = jnp.where(
                    keep, x, jnp.zeros_like(x)).astype(qd_ref.dtype)


def _finish_diff(o_ref, acc_ref, l_ref, lam, g_sub, lam_init, n_tok):
    for g in range(DIFF_KV_HEADS):
        for r in range(DIFF_REP):
            r0 = ((g * DIFF_REP + r) * 2) * n_tok
            r1 = r0 + n_tok
            o0 = acc_ref[r0:r0 + n_tok, :] / l_ref[r0:r0 + n_tok, :]
            o1 = acc_ref[r1:r1 + n_tok, :] / l_ref[r1:r1 + n_tok, :]
            d = _rms(o0 - lam * o1, g_sub, SUBLN_EPS) * (1.0 - lam_init)
            col = (g * DIFF_REP + r) * LANES
            o_ref[:, col:col + LANES] = d.astype(o_ref.dtype)


def _diff_prompt_block(qi, dq_ref, dkb_ref, dvb_ref, dkmeta_ref, dvmeta_ref,
                       lq1_ref, lk1_ref, lq2_ref, lk2_ref, g_sub_ref, o_ref,
                       qt_ref, vt_ref, vm_ref, vmt_ref, m_ref, l_ref, acc_ref,
                       s_ref, p_ref, a_ref, *, bq, bk, lam_init):
    seq = dkb_ref.shape[0]
    n_chunks = DIFF_KV_HEADS * DIFF_REP
    log2e = math.log2(math.e)

    @pl.when(qi == 0)
    def _():
        for kb in range(seq // bk):
            vt_ref[kb] = _t(dvb_ref[kb * bk:(kb + 1) * bk, :]).astype(BF16)
        vm_ref[...] = jnp.zeros(vm_ref.shape, BF16)
        vm_ref[0:N_META, :] = dvmeta_ref[...]
        vmt_ref[...] = _t(vm_ref[...]).astype(BF16)

    dim = lax.broadcasted_iota(jnp.int32, (LANES, bq), 0)
    for c in range(n_chunks):
        x = _t(dq_ref[:, c * LANES:(c + 1) * LANES] * DIFF_SCALE)
        qt_ref[c, :, 0:bq] = jnp.where(dim < DIFF_HEAD_DIM, x, 0.0).astype(BF16)
        qt_ref[c, :, bq:2 * bq] = jnp.where(dim >= DIFF_HEAD_DIM, x, 0.0).astype(BF16)

    for c in range(n_chunks):
        g = c // DIFF_REP
        s_ref[c, 0:N_META, :] = _dot(dkmeta_ref[:, g * LANES:(g + 1) * LANES], qt_ref[c])
    for c in range(n_chunks):
        s = s_ref[c, 0:N_META, :]
        m0 = jnp.max(s, axis=0, keepdims=True)
        p = jnp.exp2((s - m0) * log2e)
        m_ref[c:c + 1, :] = m0
        l_ref[c:c + 1, :] = jnp.sum(p, axis=0, keepdims=True)
        p_ref[c, 0:N_META, :] = p.astype(BF16)
        p_ref[c, N_META:LANES, :] = jnp.zeros((LANES - N_META, 2 * bq), BF16)
    for c in range(n_chunks):
        g = c // DIFF_REP
        acc_ref[c] = _dot(vmt_ref[g * LANES:(g + 1) * LANES, :], p_ref[c, 0:LANES, :])

    def attend(dk, vt, visible):
        n = dk.shape[0]
        for c in range(n_chunks):
            g = c // DIFF_REP
            s_ref[c, 0:n, :] = _dot(dk[:, g * LANES:(g + 1) * LANES], qt_ref[c])
        for c in range(n_chunks):
            s = s_ref[c, 0:n, :]
            if visible is not None:
                s = jnp.where(visible, s, NEG_INF)
            p, alpha, m_new, l_new = _online_softmax_t(
                s, m_ref[c:c + 1, :], l_ref[c:c + 1, :], log2e)
            p_ref[c, 0:n, :] = p.astype(BF16)
            a_ref[c:c + 1, :] = alpha
            m_ref[c:c + 1, :] = m_new
            l_ref[c:c + 1, :] = l_new
        for c in range(n_chunks):
            g = c // DIFF_REP
            acc_ref[c] = a_ref[c:c + 1, :] * acc_ref[c] + _dot(
                vt[g * LANES:(g + 1) * LANES, :], p_ref[c, 0:n, :])

    n_full = (qi * bq) // bk

    def body(kb, carry):
        start = pl.multiple_of(kb * bk, bk)
        attend(dkb_ref[pl.ds(start, bk), :], vt_ref[kb], None)
        return carry

    lax.fori_loop(0, n_full, body, 0)
    start = pl.multiple_of(n_full * bk, bk)
    key = start + lax.broadcasted_iota(jnp.int32, (bk, 2 * bq), 0)
    tok = qi * bq + (lax.broadcasted_iota(jnp.int32, (bk, 2 * bq), 1) & (bq - 1))
    attend(dkb_ref[pl.ds(start, bk), :], vt_ref[n_full], key <= tok)

    lam = _lambda(lq1_ref, lk1_ref, lq2_ref, lk2_ref, lam_init)
    for c in range(n_chunks):
        o = acc_ref[c] / l_ref[c:c + 1, :]
        d = (o[:, 0:bq] - lam * o[:, bq:2 * bq]).T
        d = _rms(d, g_sub_ref[...], SUBLN_EPS) * (1.0 - lam_init)
        o_ref[:, c * LANES:(c + 1) * LANES] = d.astype(o_ref.dtype)


def _paged_step(b, j, n_steps, n_total, pt_ref, qabs_ref, qpe_ref, dq_ref, c_new_ref,
                kr_new_ref, dk_new_ref, dv_new_ref, lq1_ref, lk1_ref, lq2_ref, lk2_ref,
                g_sub_ref, lat_hbm, krt_hbm, dkt_hbm, dv_hbm, mla_o_ref, diff_o_ref,
                qm_ref, qp_ref, qd_ref, kc_ref, krt_ref, dkt_ref, dv_ref, new_c_ref,
                new_kr_ref, new_dk_ref, new_dv_ref, mm_ref, lm_ref, accm_ref, md_ref,
                ld_ref, accd_ref, lat_buf, krt_buf, dkt_buf, dv_buf, sem,
                *, n_tok, pages, lam_init):
    half = DIFF_ROWS // DIFF_KV_HEADS * n_tok

    t = b * n_steps + j
    slot = lax.rem(t, PAGE_SLOTS)
    pools = ((lat_hbm, lat_buf), (krt_hbm, krt_buf), (dkt_hbm, dkt_buf), (dv_hbm, dv_buf))

    def page_copies(step, into):
        seq = step // n_steps
        first = (step - seq * n_steps) * pages
        return [pltpu.make_async_copy(src.at[0, pt_ref[seq, first + p_]],
                                      dst.at[into, p_], sem.at[into])
                for p_ in range(pages) for src, dst in pools]

    @pl.when(t == 0)
    def _():
        for ahead in range(PAGE_SLOTS - 1):
            @pl.when(ahead < n_total)
            def _():
                for copy in page_copies(ahead, ahead):
                    copy.start()

    @pl.when(t + PAGE_SLOTS - 1 < n_total)
    def _():
        for copy in page_copies(t + PAGE_SLOTS - 1, lax.rem(t + PAGE_SLOTS - 1, PAGE_SLOTS)):
            copy.start()

    for copy in page_copies(t, slot):
        copy.wait()
    page_refs = [buf.at[slot, p_] for p_ in range(pages) for _, buf in pools]

    def update(s_m, s_d, mix_m, mix_d):
        m_prev = mm_ref[...]
        m_new = jnp.maximum(m_prev, jnp.max(s_m, axis=-1, keepdims=True))
        alpha = jnp.exp(m_prev - m_new)
        p = jnp.exp(s_m - m_new)
        lm_ref[...] = alpha * lm_ref[...] + jnp.sum(p, axis=-1, keepdims=True)
        accm_ref[...] = alpha * accm_ref[...] + mix_m(p.astype(BF16))
        mm_ref[...] = m_new
        m_prev = md_ref[...]
        m_new = jnp.maximum(m_prev, jnp.max(s_d, axis=-1, keepdims=True))
        alpha = jnp.exp(m_prev - m_new)
        p = jnp.exp(s_d - m_new)
        ld_ref[...] = alpha * ld_ref[...] + jnp.sum(p, axis=-1, keepdims=True)
        p = p.astype(BF16)
        for g in range(DIFF_KV_HEADS):
            rows = slice(g * half, (g + 1) * half)
            accd_ref[rows, :] = alpha[rows] * accd_ref[rows, :] + mix_d(p[rows], g)
        md_ref[...] = m_new

    @pl.when(j == 0)
    def _():
        for h in range(MLA_HEADS):
            qm_ref[h * n_tok:(h + 1) * n_tok, :] = qabs_ref[:, h * MLA_KV_RANK:(h + 1) * MLA_KV_RANK]
            qp_ref[h * n_tok:(h + 1) * n_tok, :] = qpe_ref[:, h * MLA_ROPE:(h + 1) * MLA_ROPE]
        _fill_diff_queries(qd_ref, dq_ref, n_tok)
        mm_ref[...] = jnp.full(mm_ref.shape, NEG_INF, F32)
        lm_ref[...] = jnp.zeros(lm_ref.shape, F32)
        accm_ref[...] = jnp.zeros(accm_ref.shape, F32)
        md_ref[...] = jnp.full(md_ref.shape, NEG_INF, F32)
        ld_ref[...] = jnp.zeros(ld_ref.shape, F32)
        accd_ref[...] = jnp.zeros(accd_ref.shape, F32)
        for dst, src in ((new_c_ref, c_new_ref), (new_kr_ref, kr_new_ref),
                         (new_dk_ref, dk_new_ref), (new_dv_ref, dv_new_ref)):
            dst[...] = jnp.zeros(dst.shape, F32)
            dst[0:n_tok, :] = src[...]
        new_c = new_c_ref[...].astype(BF16)
        new_dv = new_dv_ref[...].astype(BF16)
        s_m = (_dot_nt(qm_ref[...].astype(BF16), new_c)
               + _dot_nt(qp_ref[...].astype(BF16), new_kr_ref[...].astype(BF16))) * MLA_SCALE
        s_d = _dot_nt(qd_ref[...].astype(BF16), new_dk_ref[...].astype(BF16))
        update(_causal(s_m, n_tok, 0, 0), _causal(s_d, n_tok, 0, 0),
               lambda p: _dot(p, new_c),
               lambda p, g: _dot(p, new_dv[:, g * LANES:(g + 1) * LANES]))

    for p_ in range(pages):
        span = slice(p_ * PAGE_SIZE, (p_ + 1) * PAGE_SIZE)
        kc_ref[span, :] = page_refs[4 * p_][...].astype(BF16)
        krt_ref[:, span] = page_refs[4 * p_ + 1][...].astype(BF16)
        dkt_ref[:, span] = page_refs[4 * p_ + 2][...].astype(BF16)
        for g in range(DIFF_KV_HEADS):
            dv_ref[g, span, :] = page_refs[4 * p_ + 3][
                pl.ds(g, PAGE_SIZE, stride=DIFF_KV_HEADS), :].astype(BF16)
    s_m = (_dot_nt(qm_ref[...].astype(BF16), kc_ref[...])
           + _dot(qp_ref[...].astype(BF16), krt_ref[...])) * MLA_SCALE
    s_d = _dot(qd_ref[...].astype(BF16), dkt_ref[...])
    update(s_m, s_d, lambda p: _dot(p, kc_ref[...]), lambda p, g: _dot(p, dv_ref[g]))

    @pl.when(j == n_steps - 1)
    def _():
        for h in range(MLA_HEADS):
            rows = slice(h * n_tok, (h + 1) * n_tok)
            mla_o_ref[:, h * MLA_KV_RANK:(h + 1) * MLA_KV_RANK] = accm_ref[rows, :] / lm_ref[rows, :]
        lam = _lambda(lq1_ref, lk1_ref, lq2_ref, lk2_ref, lam_init)
        _finish_diff(diff_o_ref, accd_ref, ld_ref, lam, g_sub_ref[...], lam_init, n_tok)


N_DIFF_IN, N_DIFF_SCRATCH = 10, 10
N_PAGED_IN, N_PAGED_OUT = 16, 2


def _attn_kernel(pt_ref, *refs, n_seq, n_steps, diff_kw, paged_kw):
    diff_in = refs[:N_DIFF_IN]
    paged_in = refs[N_DIFF_IN:N_DIFF_IN + N_PAGED_IN]
    outs = refs[N_DIFF_IN + N_PAGED_IN:N_DIFF_IN + N_PAGED_IN + 1 + N_PAGED_OUT]
    scratch = refs[N_DIFF_IN + N_PAGED_IN + 1 + N_PAGED_OUT:]
    qi = pl.program_id(1)
    step = pl.program_id(0) * pl.num_programs(1) + qi
    _diff_prompt_block(qi, *diff_in, outs[0], *scratch[:N_DIFF_SCRATCH], **diff_kw)

    @pl.when(step < n_seq)
    def _():
        def tile(j, carry):
            _paged_step(step, j, n_steps, n_seq * n_steps, pt_ref, *paged_in, *outs[1:],
                        *scratch[N_DIFF_SCRATCH:], **paged_kw)
            return carry

        lax.fori_loop(0, n_steps, tile, 0)


def _attn(dq, dkb, dvb, dkmeta, dvmeta, lams, g_sub, batch, seq, bq, bk,
          page_table, qabs, qpe, dq_s, c_new, kr_new, dk_new, dv_new, pools, n_tok, pages,
          lam_init):
    assert bq & (bq - 1) == 0 and bk % bq == 0 and seq % bk == 0
    nq = seq // bq
    n_chunks = DIFF_KV_HEADS * DIFF_REP
    n_seq, n_pages = page_table.shape
    assert n_seq <= batch * nq, "one sample sequence per prompt query block"
    qrow = lambda b, i, pt: (b * nq + i, 0)
    kv = lambda b, i, pt: (b, 0)
    const = lambda b, i, pt: (0, 0)
    seq_row = lambda b, i, pt: (jnp.minimum(b * nq + i, n_seq - 1), 0)
    page_specs = [pl.BlockSpec(memory_space=pl.ANY)] * len(pools)
    page_bufs = [pltpu.VMEM((PAGE_SLOTS, pages) + pool.shape[2:], pool.dtype) for pool in pools]
    n_keys = pages * PAGE_SIZE
    mla_rows = MLA_HEADS * n_tok
    diff_rows = DIFF_ROWS * n_tok
    lam_specs = [pl.BlockSpec((1, DIFF_HEAD_DIM), const)] * 4 + [
        pl.BlockSpec((1, 2 * DIFF_HEAD_DIM), const)]
    grid_spec = pltpu.PrefetchScalarGridSpec(
        num_scalar_prefetch=1,
        grid=(batch, nq),
        in_specs=[pl.BlockSpec((bq, DIFF_WIDTH), qrow),
                  pl.BlockSpec((seq, DIFF_KV_WIDTH), kv),
                  pl.BlockSpec((seq, DIFF_KV_WIDTH), kv),
                  pl.BlockSpec(dkmeta.shape, const),
                  pl.BlockSpec(dvmeta.shape, const)]
                 + lam_specs
                 + [pl.BlockSpec((n_tok, MLA_HEADS * MLA_KV_RANK), seq_row),
                    pl.BlockSpec((n_tok, MLA_HEADS * MLA_ROPE), seq_row),
                    pl.BlockSpec((n_tok, DIFF_WIDTH), seq_row),
                    pl.BlockSpec((n_tok, MLA_KV_RANK), seq_row),
                    pl.BlockSpec((n_tok, MLA_ROPE), seq_row),
                    pl.BlockSpec((n_tok, DIFF_KV_WIDTH), seq_row),
                    pl.BlockSpec((n_tok, DIFF_KV_WIDTH), seq_row)]
                 + lam_specs
                 + page_specs,
        out_specs=[pl.BlockSpec((bq, DIFF_WIDTH), qrow),
                   pl.BlockSpec((n_tok, MLA_HEADS * MLA_KV_RANK), seq_row),
                   pl.BlockSpec((n_tok, DIFF_WIDTH), seq_row)],
        scratch_shapes=[pltpu.VMEM((n_chunks, LANES, 2 * bq), BF16),
                        pltpu.VMEM((seq // bk, DIFF_KV_WIDTH, bk), BF16),
                        pltpu.VMEM((LANES, DIFF_KV_WIDTH), BF16),
                        pltpu.VMEM((DIFF_KV_WIDTH, LANES), BF16),
                        pltpu.VMEM((n_chunks, 2 * bq), F32),
                        pltpu.VMEM((n_chunks, 2 * bq), F32),
                        pltpu.VMEM((n_chunks, 2 * DIFF_HEAD_DIM, 2 * bq), F32),
                        pltpu.VMEM((n_chunks, bk, 2 * bq), F32),
                        pltpu.VMEM((n_chunks, bk, 2 * bq), BF16),
                        pltpu.VMEM((n_chunks, 2 * bq), F32),
                        pltpu.VMEM((mla_rows, MLA_KV_RANK), F32),
                        pltpu.VMEM((mla_rows, MLA_ROPE), F32),
                        pltpu.VMEM((diff_rows, DIFF_KV_WIDTH), F32),
                        pltpu.VMEM((n_keys, MLA_KV_RANK), BF16),
                        pltpu.VMEM((MLA_ROPE, n_keys), BF16),
                        pltpu.VMEM((DIFF_KV_WIDTH, n_keys), BF16),
                        pltpu.VMEM((DIFF_KV_HEADS, n_keys, 2 * DIFF_HEAD_DIM), BF16),
                        pltpu.VMEM((PAGE_SIZE, MLA_KV_RANK), F32),
                        pltpu.VMEM((PAGE_SIZE, MLA_ROPE), F32),
                        pltpu.VMEM((PAGE_SIZE, DIFF_KV_WIDTH), F32),
                        pltpu.VMEM((PAGE_SIZE, DIFF_KV_WIDTH), F32),
                        pltpu.VMEM((mla_rows, 1), F32),
                        pltpu.VMEM((mla_rows, 1), F32),
                        pltpu.VMEM((mla_rows, MLA_KV_RANK), F32),
                        pltpu.VMEM((diff_rows, 1), F32),
                        pltpu.VMEM((diff_rows, 1), F32),
                        pltpu.VMEM((diff_rows, 2 * DIFF_HEAD_DIM), F32)]
                       + page_bufs + [pltpu.SemaphoreType.DMA((PAGE_SLOTS,))],
    )
    rows = n_seq * n_tok
    return pl.pallas_call(
        functools.partial(
            _attn_kernel, n_seq=n_seq, n_steps=n_pages // pages,
            diff_kw=dict(bq=bq, bk=bk, lam_init=lam_init),
            paged_kw=dict(n_tok=n_tok, pages=pages, lam_init=lam_init)),
        grid_spec=grid_spec,
        out_shape=[jax.ShapeDtypeStruct((batch * seq, DIFF_WIDTH), BF16),
                   jax.ShapeDtypeStruct((rows, MLA_HEADS * MLA_KV_RANK), F32),
                   jax.ShapeDtypeStruct((rows, DIFF_WIDTH), F32)],
        compiler_params=pltpu.CompilerParams(
            dimension_semantics=("arbitrary", "arbitrary"), vmem_limit_bytes=VMEM_LIMIT),
        name="diff_and_paged_attn",
    )(page_table, dq, dkb, dvb, dkmeta, dvmeta, *lams, g_sub,
      qabs, qpe, dq_s, c_new, kr_new, dk_new, dv_new, *lams, g_sub, *pools)


def _out_proj_kernel(x_ref, lat_ref, diff_ref, w_uv_ref, g_mla_ref, w_out_ref, h_ref, mla_ref):
    for h in range(MLA_HEADS):
        lat = lat_ref[:, h * MLA_KV_RANK:(h + 1) * MLA_KV_RANK].astype(BF16)
        mla_ref[:, h * MLA_V:(h + 1) * MLA_V] = _dot(lat, w_uv_ref[h])
    mla = _rms(mla_ref[...], g_mla_ref[...], NORM_EPS).astype(BF16)
    h_ref[...] = (x_ref[...] + _dot(mla, w_out_ref[0:MLA_WIDTH, :])
                  + _dot(diff_ref[...].astype(BF16), w_out_ref[MLA_WIDTH:MLA_WIDTH + DIFF_WIDTH, :]))


def _out_proj(x, lat, diff, w, tm):
    rows, d = x.shape
    row = lambda i: (i, 0)
    return pl.pallas_call(
        _out_proj_kernel,
        grid=(rows // tm,),
        in_specs=[pl.BlockSpec((tm, d), row),
                  pl.BlockSpec((tm, lat.shape[1]), row),
                  pl.BlockSpec((tm, diff.shape[1]), row),
                  _const_spec(w["w_uv"].shape),
                  _const_spec((1, MLA_WIDTH)),
                  _const_spec(w["w_out"].shape)],
        out_specs=pl.BlockSpec((tm, d), row),
        out_shape=jax.ShapeDtypeStruct((rows, d), F32),
        scratch_shapes=[pltpu.VMEM((tm, MLA_WIDTH), F32)],
        compiler_params=pltpu.CompilerParams(
            dimension_semantics=("arbitrary",), vmem_limit_bytes=VMEM_LIMIT),
        name="out_proj",
    )(x, lat, diff, w["w_uv"], w["g_mla_out"], w["w_out"])


def _ffn_kernel(h_ref, g_ffn_ref, w_gate_ref, w_up_ref, w_down_ref, g_final_ref, o_ref, xn_ref):
    j = pl.program_id(1)

    @pl.when(j == 0)
    def _():
        h = h_ref[...]
        xn_ref[...] = _rms(h, g_ffn_ref[...], NORM_EPS).astype(BF16)
        o_ref[...] = h

    xn = xn_ref[...]
    gate = _dot(xn, w_gate_ref[...])
    up = _dot(xn, w_up_ref[...])
    act = (gate * (1.0 / (1.0 + jnp.exp(-gate))) * up).astype(BF16)
    o_ref[...] += _dot(act, w_down_ref[...])

    @pl.when(j == pl.num_programs(1) - 1)
    def _():
        o_ref[...] = _rms(o_ref[...], g_final_ref[...], NORM_EPS)


def _ffn(h, w, tm, th):
    rows, d = h.shape
    hidden = w["w_gate"].shape[1]
    row = lambda i, j: (i, 0)
    return pl.pallas_call(
        _ffn_kernel,
        grid=(rows // tm, hidden // th),
        in_specs=[pl.BlockSpec((tm, d), row),
                  pl.BlockSpec((1, d), lambda i, j: (0, 0)),
                  pl.BlockSpec((d, th), lambda i, j: (0, j)),
                  pl.BlockSpec((d, th), lambda i, j: (0, j)),
                  pl.BlockSpec((th, d), lambda i, j: (j, 0)),
                  pl.BlockSpec((1, d), lambda i, j: (0, 0))],
        out_specs=pl.BlockSpec((tm, d), row),
        out_shape=jax.ShapeDtypeStruct((rows, d), F32),
        scratch_shapes=[pltpu.VMEM((tm, d), BF16)],
        compiler_params=pltpu.CompilerParams(
            dimension_semantics=("arbitrary", "arbitrary"), vmem_limit_bytes=VMEM_LIMIT),
        name="ffn",
    )(h, w["g_ffn"], w["w_gate"], w["w_up"], w["w_down"], w["g_final"])


def _rope_tables(pos, rot_dim):
    half = rot_dim // 2
    inv = ROPE_THETA ** (-jnp.arange(half, dtype=F32) * 2.0 / rot_dim)
    ang = pos.astype(F32)[:, None] * inv
    d = jnp.arange(LANES) % DIFF_HEAD_DIM
    a = ang[:, d % half]
    cos = jnp.where(d < rot_dim, jnp.cos(a), 1.0)
    sin = jnp.sin(a)
    sin_lo = jnp.where(d < half, -sin, 0.0)
    sin_hi = jnp.where((d >= half) & (d < rot_dim), sin, 0.0)
    return cos.astype(F32), sin_lo.astype(F32), sin_hi.astype(F32)


def _tables(pos):
    return _rope_tables(pos, MLA_ROPE) + _rope_tables(pos, DIFF_ROT)


def _row_block(rows, target):
    tm = min(rows, target)
    while rows % tm:
        tm //= 2
    return tm


def kernel(x_prompt, x_sample, cache_mla_latent, cache_mla_krope, cache_diff_k, cache_diff_v,
           page_table, meta_tokens, g_attn, w_in, g_q_a, w_q_b, g_kv_a, w_uk, w_uv, g_mla_out,
           lambda_q1, lambda_k1, lambda_q2, lambda_k2, g_subln, w_out, g_ffn, w_gate, w_up,
           w_down, g_final):
    batch, seq, d = x_prompt.shape
    n_seq, n_tok, _ = x_sample.shape
    depth = w_in.shape[0]
    assert depth == 1, "single-layer trunk"
    n_phys = cache_mla_latent.shape[1]
    past_len = page_table.shape[1] * PAGE_SIZE
    lam_init = 0.8 - 0.6 * math.exp(-0.3 * 0)

    wi = w_in[0]
    o = [0, MLA_Q_RANK, MLA_Q_RANK + MLA_KV_RANK, MLA_Q_RANK + MLA_KV_RANK + MLA_ROPE]
    o.append(o[3] + DIFF_WIDTH)
    o.append(o[4] + DIFF_KV_WIDTH)
    o.append(o[5] + DIFF_KV_WIDTH)
    w_in_r = jnp.concatenate(
        [wi[:, o[0]:o[2]], wi[:, o[3]:o[6]], wi[:, o[2]:o[3]],
         jnp.zeros((d, LANES - MLA_ROPE), wi.dtype)], axis=1).astype(BF16)
    wq = w_q_b[0]
    w_qb = jnp.concatenate(
        [wq[:, :, :MLA_QK_NOPE].reshape(MLA_Q_RANK, -1),
         wq[:, :, MLA_QK_NOPE:].reshape(MLA_Q_RANK, -1)], axis=1).astype(BF16)
    w = {
        "g_attn": g_attn[0][None], "w_in": w_in_r, "g_q_a": g_q_a[0][None],
        "g_kv_a": g_kv_a[0][None], "w_qb": w_qb,
        "w_uk": jnp.transpose(w_uk[0], (1, 2, 0)).astype(BF16),
        "w_uv": jnp.transpose(w_uv[0], (1, 0, 2)).astype(BF16),
        "g_mla_out": g_mla_out[0][None], "w_out": w_out[0].astype(BF16),
        "g_ffn": g_ffn[0][None], "w_gate": w_gate[0].astype(BF16),
        "w_up": w_up[0].astype(BF16), "w_down": w_down[0].astype(BF16),
        "g_final": g_final[None],
    }
    lams = (lambda_q1, lambda_k1, lambda_q2, lambda_k2)
    g_sub = g_subln[0][None]

    xp = x_prompt.reshape(batch * seq, d)
    xs = x_sample.reshape(n_seq * n_tok, d)
    tm_p = _row_block(seq, 256)
    tm_s = _row_block(n_seq * n_tok, 256)
    tab_p = _tables(N_META + jnp.arange(seq))
    tab_s = _tables(jnp.tile(past_len + jnp.arange(n_tok), tm_s // n_tok))
    tab_m = _tables(jnp.arange(N_META))
    (qabs_p, qpe_p, dq_p, c_p, kr_p, dk_p, dv_p, cb_p, krb_p, dkb_p, dvb_p) = _proj(
        xp, tab_p, tm_p, BF16, w)
    (qabs_s, qpe_s, dq_s, c_s, kr_s, dk_s, dv_s, _, _, _, _) = _proj(xs, tab_s, tm_s, F32, w)
    (_, _, _, c_m, kr_m, dk_m, dv_m, cb_m, krb_m, dkb_m, dvb_m) = _proj(
        meta_tokens.astype(F32), tab_m, N_META, BF16, w)

    lat_p = _mla_attn(qabs_p, qpe_p, cb_p, krb_p, cb_m, krb_m, batch, seq, _row_block(seq, 256))

    pools = (cache_mla_latent,
             jnp.swapaxes(cache_mla_krope, 2, 3),
             jnp.transpose(cache_diff_k, (0, 1, 3, 4, 5, 2)).reshape(
                 depth, n_phys, DIFF_KV_WIDTH, PAGE_SIZE),
             cache_diff_v.reshape(depth, n_phys, PAGE_SIZE * DIFF_KV_HEADS, 2 * DIFF_HEAD_DIM))
    pages = _row_block(page_table.shape[1], 16)
    diff_p, lat_s, diff_s = _attn(
        dq_p, dkb_p, dvb_p, dkb_m, dvb_m, lams, g_sub, batch, seq,
        _row_block(seq, 128), _row_block(seq, 256),
        page_table, qabs_s, qpe_s, dq_s, c_s, kr_s, dk_s, dv_s, pools, n_tok, pages, lam_init)

    h_p = _out_proj(xp, lat_p, diff_p, w, _row_block(batch * seq, 512))
    h_s = _out_proj(xs, lat_s, diff_s, w, _row_block(n_seq * n_tok, 512))
    hidden = w_gate.shape[2]
    th = 512 if hidden % 512 == 0 else hidden
    y_p = _ffn(h_p, w, _row_block(batch * seq, 512), th).reshape(batch, seq, d)
    y_s = _ffn(h_s, w, _row_block(n_seq * n_tok, 512), th).reshape(n_seq, n_tok, d)

    def with_meta(meta, real):
        n = real.shape[-1]
        full = jnp.concatenate(
            [jnp.broadcast_to(meta[None], (batch, N_META, n)), real.reshape(batch, seq, n)], axis=1)
        return full[None]

    t = seq + N_META
    kv_shape = (DIFF_KV_HEADS, 2, DIFF_HEAD_DIM)
    v_shape = (DIFF_KV_HEADS, 2 * DIFF_HEAD_DIM)
    return (y_p, y_s,
            with_meta(c_m, c_p), with_meta(kr_m, kr_p),
            with_meta(dk_m, dk_p).reshape((1, batch, t) + kv_shape),
            with_meta(dv_m, dv_p).reshape((1, batch, t) + v_shape),
            c_s.reshape(1, n_seq, n_tok, -1), kr_s.reshape(1, n_seq, n_tok, -1),
            dk_s.reshape((1, n_seq, n_tok) + kv_shape),
            dv_s.reshape((1, n_seq, n_tok) + v_shape))
```

```python
import functools
import math

import jax
import jax.numpy as jnp
from jax import lax
from jax.experimental import pallas as pl
from jax.experimental.pallas import tpu as pltpu

F32 = jnp.float32
BF16 = jnp.bfloat16

N_META = 16
ROPE_THETA = 500000.0
NORM_EPS = 1e-6
SUBLN_EPS = 1e-5
NEG_INF = -1e30
PAGE_SIZE = 128

MLA_HEADS = 8
MLA_QK_NOPE = 128
MLA_ROPE = 64
MLA_V = 128
MLA_Q_RANK = 512
MLA_KV_RANK = 256
MLA_WIDTH = MLA_HEADS * MLA_V
MLA_SCALE = (MLA_QK_NOPE + MLA_ROPE) ** -0.5

DIFF_HEADS = 8
DIFF_KV_HEADS = 2
DIFF_REP = DIFF_HEADS // DIFF_KV_HEADS
DIFF_HEAD_DIM = 64
DIFF_ROT = DIFF_HEAD_DIM // 4
DIFF_WIDTH = DIFF_HEADS * 2 * DIFF_HEAD_DIM
DIFF_KV_WIDTH = DIFF_KV_HEADS * 2 * DIFF_HEAD_DIM
DIFF_SCALE = DIFF_HEAD_DIM ** -0.5
DIFF_ROWS = DIFF_KV_HEADS * DIFF_REP * 2

PAGE_SLOTS = 4
LANES = 128
VMEM_LIMIT = 56 * 1024 * 1024

OFF_QA = 0
OFF_KVA = OFF_QA + MLA_Q_RANK
OFF_DQ = OFF_KVA + MLA_KV_RANK
OFF_DK = OFF_DQ + DIFF_WIDTH
OFF_DV = OFF_DK + DIFF_KV_WIDTH
OFF_KR = OFF_DV + DIFF_KV_WIDTH
IN_WIDTH_PADDED = OFF_KR + LANES


def _rms(x, g, eps):
    return x * lax.rsqrt(jnp.mean(x * x, axis=-1, keepdims=True) + eps) * g


def _dot(a, b):
    return jnp.dot(a, b, preferred_element_type=F32)


def _dot_nt(a, b):
    return lax.dot_general(a, b, (((1,), (1,)), ((), ())), preferred_element_type=F32)


def _rope_chunk(x, cos, sin_lo, sin_hi, half):
    return (x * cos + pltpu.roll(x, LANES - half, 1) * sin_lo
            + pltpu.roll(x, half, 1) * sin_hi)


def _causal(s, n_tok, tok0, col0):
    assert n_tok & (n_tok - 1) == 0
    tok = tok0 + (lax.broadcasted_iota(jnp.int32, s.shape, 0) & (n_tok - 1))
    col = col0 + lax.broadcasted_iota(jnp.int32, s.shape, 1)
    return jnp.where(col <= tok, s, NEG_INF)


def _const_spec(shape):
    return pl.BlockSpec(shape, lambda *_: (0,) * len(shape))


def _proj_kernel(x_ref, g_attn_ref, w_in_ref, g_q_ref, g_kv_ref, w_qb_ref, w_uk_ref,
                 cos_m_ref, slo_m_ref, shi_m_ref, cos_d_ref, slo_d_ref, shi_d_ref,
                 qabs_ref, qpe_ref, dq_ref, c_ref, kr_ref, dk_ref, dv_ref,
                 cb_ref, krb_ref, dkb_ref, dvb_ref):
    xn = _rms(x_ref[...], g_attn_ref[...], NORM_EPS).astype(BF16)
    z = _dot(xn, w_in_ref[...])

    cos_m, slo_m, shi_m = cos_m_ref[...], slo_m_ref[...], shi_m_ref[...]
    cos_d, slo_d, shi_d = cos_d_ref[...], slo_d_ref[...], shi_d_ref[...]

    c = _rms(z[:, OFF_KVA:OFF_KVA + MLA_KV_RANK], g_kv_ref[...], NORM_EPS)
    c_ref[...] = c
    cb_ref[...] = c.astype(BF16)
    kr = _rope_chunk(z[:, OFF_KR:OFF_KR + LANES], cos_m, slo_m, shi_m, MLA_ROPE // 2)
    kr_ref[...] = kr[:, :MLA_ROPE]
    krb_ref[...] = kr[:, :MLA_ROPE].astype(BF16)

    qn = _rms(z[:, OFF_QA:OFF_QA + MLA_Q_RANK], g_q_ref[...], NORM_EPS).astype(BF16)
    q = _dot(qn, w_qb_ref[...])
    nope_w = MLA_HEADS * MLA_QK_NOPE
    for h in range(MLA_HEADS):
        qh = q[:, h * MLA_QK_NOPE:(h + 1) * MLA_QK_NOPE].astype(BF16)
        qabs_ref[:, h * MLA_KV_RANK:(h + 1) * MLA_KV_RANK] = _dot(qh, w_uk_ref[h]).astype(qabs_ref.dtype)
    for j in range(MLA_HEADS * MLA_ROPE // LANES):
        qc = q[:, nope_w + j * LANES:nope_w + (j + 1) * LANES]
        qpe_ref[:, j * LANES:(j + 1) * LANES] = _rope_chunk(
            qc, cos_m, slo_m, shi_m, MLA_ROPE // 2).astype(qpe_ref.dtype)

    for j in range(DIFF_WIDTH // LANES):
        xc = z[:, OFF_DQ + j * LANES:OFF_DQ + (j + 1) * LANES]
        dq_ref[:, j * LANES:(j + 1) * LANES] = _rope_chunk(
            xc, cos_d, slo_d, shi_d, DIFF_ROT // 2).astype(dq_ref.dtype)
    for j in range(DIFF_KV_WIDTH // LANES):
        xc = z[:, OFF_DK + j * LANES:OFF_DK + (j + 1) * LANES]
        dk = _rope_chunk(xc, cos_d, slo_d, shi_d, DIFF_ROT // 2)
        dk_ref[:, j * LANES:(j + 1) * LANES] = dk
        dkb_ref[:, j * LANES:(j + 1) * LANES] = dk.astype(BF16)
    dv = z[:, OFF_DV:OFF_DV + DIFF_KV_WIDTH]
    dv_ref[...] = dv
    dvb_ref[...] = dv.astype(BF16)


def _proj(x, tables, tm, q_dtype, w):
    rows, d = x.shape
    t_tab = tables[0].shape[0]
    n_tab = t_tab // tm
    row = lambda i: (i, 0)
    tab = lambda i: (i % n_tab, 0)
    out_widths = [(MLA_HEADS * MLA_KV_RANK, q_dtype), (MLA_HEADS * MLA_ROPE, q_dtype),
                  (DIFF_WIDTH, q_dtype), (MLA_KV_RANK, F32), (MLA_ROPE, F32),
                  (DIFF_KV_WIDTH, F32), (DIFF_KV_WIDTH, F32), (MLA_KV_RANK, BF16),
                  (MLA_ROPE, BF16), (DIFF_KV_WIDTH, BF16), (DIFF_KV_WIDTH, BF16)]
    return pl.pallas_call(
        _proj_kernel,
        grid=(rows // tm,),
        in_specs=[pl.BlockSpec((tm, d), row),
                  _const_spec((1, d)),
                  _const_spec(w["w_in"].shape),
                  _const_spec((1, MLA_Q_RANK)),
                  _const_spec((1, MLA_KV_RANK)),
                  _const_spec(w["w_qb"].shape),
                  _const_spec(w["w_uk"].shape)]
                 + [pl.BlockSpec((tm, LANES), tab)] * 6,
        out_specs=[pl.BlockSpec((tm, n), row) for n, _ in out_widths],
        out_shape=[jax.ShapeDtypeStruct((rows, n), dt) for n, dt in out_widths],
        compiler_params=pltpu.CompilerParams(
            dimension_semantics=("arbitrary",), vmem_limit_bytes=VMEM_LIMIT),
        name="proj",
    )(x, w["g_attn"], w["w_in"], w["g_q_a"], w["g_kv_a"], w["w_qb"], w["w_uk"], *tables)


def _t(x):
    return x.astype(F32).T


def _online_softmax_t(s, m_prev, l_prev, log2_scale):
    m_new = jnp.maximum(m_prev, jnp.max(s, axis=0, keepdims=True))
    alpha = jnp.exp2((m_prev - m_new) * log2_scale)
    p = jnp.exp2((s - m_new) * log2_scale)
    return p, alpha, m_new, alpha * l_prev + jnp.sum(p, axis=0, keepdims=True)


def _mla_attn_kernel(qabs_ref, qpe_ref, cb_ref, krb_ref, cmeta_ref, krmeta_ref, o_ref,
                     qt_ref, ct_ref, cm_ref, cmt_ref, m_ref, l_ref, acc_ref, s_ref, p_ref,
                     a_ref, *, bq):
    qi = pl.program_id(1)
    seq = cb_ref.shape[0]
    log2_scale = MLA_SCALE * math.log2(math.e)

    @pl.when(qi == 0)
    def _():
        for kb in range(seq // bq):
            ct_ref[kb] = _t(cb_ref[kb * bq:(kb + 1) * bq, :]).astype(BF16)
        cm_ref[...] = jnp.zeros(cm_ref.shape, BF16)
        cm_ref[0:N_META, :] = cmeta_ref[...]
        cmt_ref[...] = _t(cm_ref[...]).astype(BF16)

    for h in range(MLA_HEADS):
        qt_ref[h, 0:MLA_KV_RANK, :] = _t(
            qabs_ref[:, h * MLA_KV_RANK:(h + 1) * MLA_KV_RANK]).astype(BF16)
    per_chunk = LANES // MLA_ROPE
    for j in range(MLA_HEADS // per_chunk):
        t = _t(qpe_ref[:, j * LANES:(j + 1) * LANES]).astype(BF16)
        for i in range(per_chunk):
            qt_ref[j * per_chunk + i, MLA_KV_RANK:MLA_KV_RANK + MLA_ROPE, :] = (
                t[i * MLA_ROPE:(i + 1) * MLA_ROPE])
    for h in range(MLA_HEADS):
        s_ref[h, 0:N_META, :] = (
            _dot(cmeta_ref[...], qt_ref[h, 0:MLA_KV_RANK, :])
            + _dot(krmeta_ref[...], qt_ref[h, MLA_KV_RANK:MLA_KV_RANK + MLA_ROPE, :]))
    for h in range(MLA_HEADS):
        s = s_ref[h, 0:N_META, :]
        m0 = jnp.max(s, axis=0, keepdims=True)
        p = jnp.exp2((s - m0) * log2_scale)
        m_ref[h:h + 1, :] = m0
        l_ref[h:h + 1, :] = jnp.sum(p, axis=0, keepdims=True)
        p_ref[h, 0:N_META, :] = p.astype(BF16)
        p_ref[h, N_META:LANES, :] = jnp.zeros((LANES - N_META, bq), BF16)
    for h in range(MLA_HEADS):
        acc_ref[h] = _dot(cmt_ref[...], p_ref[h, 0:LANES, :])

    def attend(kc, kkr, ct, visible):
        n = kc.shape[0]
        for h in range(MLA_HEADS):
            s_ref[h, 0:n, :] = (_dot(kc, qt_ref[h, 0:MLA_KV_RANK, :])
                                + _dot(kkr, qt_ref[h, MLA_KV_RANK:MLA_KV_RANK + MLA_ROPE, :]))
        for h in range(MLA_HEADS):
            s = s_ref[h, 0:n, :]
            if visible is not None:
                s = jnp.where(visible, s, NEG_INF)
            p, alpha, m_new, l_new = _online_softmax_t(
                s, m_ref[h:h + 1, :], l_ref[h:h + 1, :], log2_scale)
            p_ref[h, 0:n, :] = p.astype(BF16)
            a_ref[h:h + 1, :] = alpha
            m_ref[h:h + 1, :] = m_new
            l_ref[h:h + 1, :] = l_new
        for h in range(MLA_HEADS):
            acc_ref[h] = a_ref[h:h + 1, :] * acc_ref[h] + _dot(ct, p_ref[h, 0:n, :])

    def body(kb, carry):
        start = pl.multiple_of(kb * bq, bq)
        attend(cb_ref[pl.ds(start, bq), :], krb_ref[pl.ds(start, bq), :], ct_ref[kb], None)
        return carry

    lax.fori_loop(0, qi, body, 0)
    start = pl.multiple_of(qi * bq, bq)
    key = lax.broadcasted_iota(jnp.int32, (bq, bq), 0)
    tok = lax.broadcasted_iota(jnp.int32, (bq, bq), 1)
    attend(cb_ref[pl.ds(start, bq), :], krb_ref[pl.ds(start, bq), :], ct_ref[qi], key <= tok)

    for h in range(MLA_HEADS):
        o_ref[:, h * MLA_KV_RANK:(h + 1) * MLA_KV_RANK] = (
            acc_ref[h] / l_ref[h:h + 1, :]).T.astype(o_ref.dtype)


def _mla_attn(qabs, qpe, cb, krb, cmeta, krmeta, batch, seq, bq):
    nq = seq // bq
    qrow = lambda b, i: (b * nq + i, 0)
    kv = lambda b, i: (b, 0)
    return pl.pallas_call(
        functools.partial(_mla_attn_kernel, bq=bq),
        grid=(batch, nq),
        in_specs=[pl.BlockSpec((bq, MLA_HEADS * MLA_KV_RANK), qrow),
                  pl.BlockSpec((bq, MLA_HEADS * MLA_ROPE), qrow),
                  pl.BlockSpec((seq, MLA_KV_RANK), kv),
                  pl.BlockSpec((seq, MLA_ROPE), kv),
                  _const_spec(cmeta.shape),
                  _const_spec(krmeta.shape)],
        out_specs=pl.BlockSpec((bq, MLA_HEADS * MLA_KV_RANK), qrow),
        out_shape=jax.ShapeDtypeStruct((batch * seq, MLA_HEADS * MLA_KV_RANK), BF16),
        scratch_shapes=[pltpu.VMEM((MLA_HEADS, MLA_KV_RANK + MLA_ROPE, bq), BF16),
                        pltpu.VMEM((nq, MLA_KV_RANK, bq), BF16),
                        pltpu.VMEM((LANES, MLA_KV_RANK), BF16),
                        pltpu.VMEM((MLA_KV_RANK, LANES), BF16),
                        pltpu.VMEM((MLA_HEADS, bq), F32),
                        pltpu.VMEM((MLA_HEADS, bq), F32),
                        pltpu.VMEM((MLA_HEADS, MLA_KV_RANK, bq), F32),
                        pltpu.VMEM((MLA_HEADS, bq, bq), F32),
                        pltpu.VMEM((MLA_HEADS, bq, bq), BF16),
                        pltpu.VMEM((MLA_HEADS, bq), F32)],
        compiler_params=pltpu.CompilerParams(
            dimension_semantics=("arbitrary", "arbitrary"), vmem_limit_bytes=VMEM_LIMIT),
        name="mla_attn",
    )(qabs, qpe, cb, krb, cmeta, krmeta)


def _lambda(lq1_ref, lk1_ref, lq2_ref, lk2_ref, lam_init):
    a = jnp.sum(lq1_ref[...] * lk1_ref[...], axis=-1, keepdims=True)
    b = jnp.sum(lq2_ref[...] * lk2_ref[...], axis=-1, keepdims=True)
    return jnp.exp(a) - jnp.exp(b) + lam_init


def _fill_diff_queries(qd_ref, dq_ref, n_tok):
    qd_ref[...] = jnp.zeros(qd_ref.shape, qd_ref.dtype)
    lane = lax.broadcasted_iota(jnp.int32, (n_tok, LANES), 1)
    for g in range(DIFF_KV_HEADS):
        for r in range(DIFF_REP):
            col = (g * DIFF_REP + r) * LANES
            x = dq_ref[:, col:col + LANES] * DIFF_SCALE
            for m in range(2):
                row = ((g * DIFF_REP + r) * 2 + m) * n_tok
                keep = (lane < DIFF_HEAD_DIM) if m == 0 else (lane >= DIFF_HEAD_DIM)
                qd_ref[row:row + n_tok, g * LANES:(g + 1) * LANES] = jnp.where(
                    keep, x, jnp.zeros_like(x)).astype(qd_ref.dtype)


def _finish_diff(o_ref, acc_ref, l_ref, lam, g_sub, lam_init, n_tok):
    for g in range(DIFF_KV_HEADS):
        for r in range(DIFF_REP):
            r0 = ((g * DIFF_REP + r) * 2) * n_tok
            r1 = r0 + n_tok
            o0 = acc_ref[r0:r0 + n_tok, :] / l_ref[r0:r0 + n_tok, :]
            o1 = acc_ref[r1:r1 + n_tok, :] / l_ref[r1:r1 + n_tok, :]
            d = _rms(o0 - lam * o1, g_sub, SUBLN_EPS) * (1.0 - lam_init)
            col = (g * DIFF_REP + r) * LANES
            o_ref[:, col:col + LANES] = d.astype(o_ref.dtype)


def _diff_prompt_block(qi, dq_ref, dkb_ref, dvb_ref, dkmeta_ref, dvmeta_ref,
                       lq1_ref, lk1_ref, lq2_ref, lk2_ref, g_sub_ref, o_ref,
                       qt_ref, vt_ref, vm_ref, vmt_ref, m_ref, l_ref, acc_ref,
                       s_ref, p_ref, a_ref, *, bq, bk, lam_init):
    seq = dkb_ref.shape[0]
    n_chunks = DIFF_KV_HEADS * DIFF_REP
    log2e = math.log2(math.e)

    @pl.when(qi == 0)
    def _():
        for kb in range(seq // bk):
            vt_ref[kb] = _t(dvb_ref[kb * bk:(kb + 1) * bk, :]).astype(BF16)
        vm_ref[...] = jnp.zeros(vm_ref.shape, BF16)
        vm_ref[0:N_META, :] = dvmeta_ref[...]
        vmt_ref[...] = _t(vm_ref[...]).astype(BF16)

    dim = lax.broadcasted_iota(jnp.int32, (LANES, bq), 0)
    for c in range(n_chunks):
        x = _t(dq_ref[:, c * LANES:(c + 1) * LANES] * DIFF_SCALE)
        qt_ref[c, :, 0:bq] = jnp.where(dim < DIFF_HEAD_DIM, x, 0.0).astype(BF16)
        qt_ref[c, :, bq:2 * bq] = jnp.where(dim >= DIFF_HEAD_DIM, x, 0.0).astype(BF16)

    for c in range(n_chunks):
        g = c // DIFF_REP
        s_ref[c, 0:N_META, :] = _dot(dkmeta_ref[:, g * LANES:(g + 1) * LANES], qt_ref[c])
    for c in range(n_chunks):
        s = s_ref[c, 0:N_META, :]
        m0 = jnp.max(s, axis=0, keepdims=True)
        p = jnp.exp2((s - m0) * log2e)
        m_ref[c:c + 1, :] = m0
        l_ref[c:c + 1, :] = jnp.sum(p, axis=0, keepdims=True)
        p_ref[c, 0:N_META, :] = p.astype(BF16)
        p_ref[c, N_META:LANES, :] = jnp.zeros((LANES - N_META, 2 * bq), BF16)
    for c in range(n_chunks):
        g = c // DIFF_REP
        acc_ref[c] = _dot(vmt_ref[g * LANES:(g + 1) * LANES, :], p_ref[c, 0:LANES, :])

    def attend(dk, vt, visible):
        n = dk.shape[0]
        for c in range(n_chunks):
            g = c // DIFF_REP
            s_ref[c, 0:n, :] = _dot(dk[:, g * LANES:(g + 1) * LANES], qt_ref[c])
        for c in range(n_chunks):
            s = s_ref[c, 0:n, :]
            if visible is not None:
                s = jnp.where(visible, s, NEG_INF)
            p, alpha, m_new, l_new = _online_softmax_t(
                s, m_ref[c:c + 1, :], l_ref[c:c + 1, :], log2e)
            p_ref[c, 0:n, :] = p.astype(BF16)
            a_ref[c:c + 1, :] = alpha
            m_ref[c:c + 1, :] = m_new
            l_ref[c:c + 1, :] = l_new
        for c in range(n_chunks):
            g = c // DIFF_REP
            acc_ref[c] = a_ref[c:c + 1, :] * acc_ref[c] + _dot(
                vt[g * LANES:(g + 1) * LANES, :], p_ref[c, 0:n, :])

    n_full = (qi * bq) // bk

    def body(kb, carry):
        start = pl.multiple_of(kb * bk, bk)
        attend(dkb_ref[pl.ds(start, bk), :], vt_ref[kb], None)
        return carry

    lax.fori_loop(0, n_full, body, 0)
    start = pl.multiple_of(n_full * bk, bk)
    key = start + lax.broadcasted_iota(jnp.int32, (bk, 2 * bq), 0)
    tok = qi * bq + (lax.broadcasted_iota(jnp.int32, (bk, 2 * bq), 1) & (bq - 1))
    attend(dkb_ref[pl.ds(start, bk), :], vt_ref[n_full], key <= tok)

    lam = _lambda(lq1_ref, lk1_ref, lq2_ref, lk2_ref, lam_init)
    for c in range(n_chunks):
        o = acc_ref[c] / l_ref[c:c + 1, :]
        d = (o[:, 0:bq] - lam * o[:, bq:2 * bq]).T
        d = _rms(d, g_sub_ref[...], SUBLN_EPS) * (1.0 - lam_init)
        o_ref[:, c * LANES:(c + 1) * LANES] = d.astype(o_ref.dtype)


def _paged_step(b, j, n_steps, n_total, pt_ref, qabs_ref, qpe_ref, dq_ref, c_new_ref,
                kr_new_ref, dk_new_ref, dv_new_ref, lq1_ref, lk1_ref, lq2_ref, lk2_ref,
                g_sub_ref, lat_hbm, krt_hbm, dkt_hbm, dv_hbm, mla_o_ref, diff_o_ref,
                qm_ref, qp_ref, qd_ref, kc_ref, krt_ref, dkt_ref, dv_ref, new_c_ref,
                new_kr_ref, new_dk_ref, new_dv_ref, mm_ref, lm_ref, accm_ref, md_ref,
                ld_ref, accd_ref, lat_buf, krt_buf, dkt_buf, dv_buf, sem,
                *, n_tok, pages, lam_init):
    half = DIFF_ROWS // DIFF_KV_HEADS * n_tok

    t = b * n_steps + j
    slot = lax.rem(t, PAGE_SLOTS)
    pools = ((lat_hbm, lat_buf), (krt_hbm, krt_buf), (dkt_hbm, dkt_buf), (dv_hbm, dv_buf))

    def page_copies(step, into, for_wait=False):
        if for_wait:
            return [pltpu.make_async_copy(src.at[0, pl.ds(0, pages)], dst.at[into], sem.at[into])
                    for src, dst in pools]
        seq = step // n_steps
        first = (step - seq * n_steps) * pages
        ids = [pt_ref[seq, first + p_] for p_ in range(pages)]
        return [pltpu.make_async_copy(src.at[0, ids[p_]], dst.at[into, p_], sem.at[into])
                for p_ in range(pages) for src, dst in pools]

    @pl.when(t == 0)
    def _():
        for ahead in range(PAGE_SLOTS - 1):
            @pl.when(ahead < n_total)
            def _():
                for copy in page_copies(ahead, ahead):
                    copy.start()

    @pl.when(t + PAGE_SLOTS - 1 < n_total)
    def _():
        for copy in page_copies(t + PAGE_SLOTS - 1, lax.rem(t + PAGE_SLOTS - 1, PAGE_SLOTS)):
            copy.start()

    for copy in page_copies(t, slot, for_wait=True):
        copy.wait()
    page_refs = [buf.at[slot, p_] for p_ in range(pages) for _, buf in pools]

    def update(s_m, s_d, mix_m, mix_d):
        m_prev = mm_ref[...]
        m_new = jnp.maximum(m_prev, jnp.max(s_m, axis=-1, keepdims=True))
        alpha = jnp.exp(m_prev - m_new)
        p = jnp.exp(s_m - m_new)
        lm_ref[...] = alpha * lm_ref[...] + jnp.sum(p, axis=-1, keepdims=True)
        accm_ref[...] = alpha * accm_ref[...] + mix_m(p.astype(BF16))
        mm_ref[...] = m_new
        m_prev = md_ref[...]
        m_new = jnp.maximum(m_prev, jnp.max(s_d, axis=-1, keepdims=True))
        alpha = jnp.exp(m_prev - m_new)
        p = jnp.exp(s_d - m_new)
        ld_ref[...] = alpha * ld_ref[...] + jnp.sum(p, axis=-1, keepdims=True)
        p = p.astype(BF16)
        for g in range(DIFF_KV_HEADS):
            rows = slice(g * half, (g + 1) * half)
            accd_ref[rows, :] = alpha[rows] * accd_ref[rows, :] + mix_d(p[rows], g)
        md_ref[...] = m_new

    @pl.when(j == 0)
    def _():
        for h in range(MLA_HEADS):
            qm_ref[h * n_tok:(h + 1) * n_tok, :] = qabs_ref[:, h * MLA_KV_RANK:(h + 1) * MLA_KV_RANK]
            qp_ref[h * n_tok:(h + 1) * n_tok, :] = qpe_ref[:, h * MLA_ROPE:(h + 1) * MLA_ROPE]
        _fill_diff_queries(qd_ref, dq_ref, n_tok)
        mm_ref[...] = jnp.full(mm_ref.shape, NEG_INF, F32)
        lm_ref[...] = jnp.zeros(lm_ref.shape, F32)
        accm_ref[...] = jnp.zeros(accm_ref.shape, F32)
        md_ref[...] = jnp.full(md_ref.shape, NEG_INF, F32)
        ld_ref[...] = jnp.zeros(ld_ref.shape, F32)
        accd_ref[...] = jnp.zeros(accd_ref.shape, F32)
        for dst, src in ((new_c_ref, c_new_ref), (new_kr_ref, kr_new_ref),
                         (new_dk_ref, dk_new_ref), (new_dv_ref, dv_new_ref)):
            dst[...] = jnp.zeros(dst.shape, F32)
            dst[0:n_tok, :] = src[...]
        new_c = new_c_ref[...].astype(BF16)
        new_dv = new_dv_ref[...].astype(BF16)
        s_m = (_dot_nt(qm_ref[...].astype(BF16), new_c)
               + _dot_nt(qp_ref[...].astype(BF16), new_kr_ref[...].astype(BF16))) * MLA_SCALE
        s_d = _dot_nt(qd_ref[...].astype(BF16), new_dk_ref[...].astype(BF16))
        update(_causal(s_m, n_tok, 0, 0), _causal(s_d, n_tok, 0, 0),
               lambda p: _dot(p, new_c),
               lambda p, g: _dot(p, new_dv[:, g * LANES:(g + 1) * LANES]))

    for p_ in range(pages):
        span = slice(p_ * PAGE_SIZE, (p_ + 1) * PAGE_SIZE)
        kc_ref[span, :] = page_refs[4 * p_][...].astype(BF16)
        krt_ref[:, span] = page_refs[4 * p_ + 1][...].astype(BF16)
        dkt_ref[:, span] = page_refs[4 * p_ + 2][...].astype(BF16)
        for g in range(DIFF_KV_HEADS):
            dv_ref[g, span, :] = page_refs[4 * p_ + 3][
                pl.ds(g, PAGE_SIZE, stride=DIFF_KV_HEADS), :].astype(BF16)
    s_m = (_dot_nt(qm_ref[...].astype(BF16), kc_ref[...])
           + _dot(qp_ref[...].astype(BF16), krt_ref[...])) * MLA_SCALE
    s_d = _dot(qd_ref[...].astype(BF16), dkt_ref[...])
    update(s_m, s_d, lambda p: _dot(p, kc_ref[...]), lambda p, g: _dot(p, dv_ref[g]))

    @pl.when(j == n_steps - 1)
    def _():
        for h in range(MLA_HEADS):
            rows = slice(h * n_tok, (h + 1) * n_tok)
            mla_o_ref[:, h * MLA_KV_RANK:(h + 1) * MLA_KV_RANK] = accm_ref[rows, :] / lm_ref[rows, :]
        lam = _lambda(lq1_ref, lk1_ref, lq2_ref, lk2_ref, lam_init)
        _finish_diff(diff_o_ref, accd_ref, ld_ref, lam, g_sub_ref[...], lam_init, n_tok)


N_DIFF_IN, N_DIFF_SCRATCH = 10, 10
N_PAGED_IN, N_PAGED_OUT = 16, 2


def _attn_kernel(pt_ref, *refs, n_seq, n_steps, diff_kw, paged_kw):
    diff_in = refs[:N_DIFF_IN]
    paged_in = refs[N_DIFF_IN:N_DIFF_IN + N_PAGED_IN]
    outs = refs[N_DIFF_IN + N_PAGED_IN:N_DIFF_IN + N_PAGED_IN + 1 + N_PAGED_OUT]
    scratch = refs[N_DIFF_IN + N_PAGED_IN + 1 + N_PAGED_OUT:]
    qi = pl.program_id(1)
    step = pl.program_id(0) * pl.num_programs(1) + qi
    _diff_prompt_block(qi, *diff_in, outs[0], *scratch[:N_DIFF_SCRATCH], **diff_kw)

    @pl.when(step < n_seq)
    def _():
        def tile(j, carry):
            _paged_step(step, j, n_steps, n_seq * n_steps, pt_ref, *paged_in, *outs[1:],
                        *scratch[N_DIFF_SCRATCH:], **paged_kw)
            return carry

        lax.fori_loop(0, n_steps, tile, 0)


def _attn(dq, dkb, dvb, dkmeta, dvmeta, lams, g_sub, batch, seq, bq, bk,
          page_table, qabs, qpe, dq_s, c_new, kr_new, dk_new, dv_new, pools, n_tok, pages,
          lam_init):
    assert bq & (bq - 1) == 0 and bk % bq == 0 and seq % bk == 0
    nq = seq // bq
    n_chunks = DIFF_KV_HEADS * DIFF_REP
    n_seq, n_pages = page_table.shape
    assert n_seq <= batch * nq, "one sample sequence per prompt query block"
    qrow = lambda b, i, pt: (b * nq + i, 0)
    kv = lambda b, i, pt: (b, 0)
    const = lambda b, i, pt: (0, 0)
    seq_row = lambda b, i, pt: (jnp.minimum(b * nq + i, n_seq - 1), 0)
    page_specs = [pl.BlockSpec(memory_space=pl.ANY)] * len(pools)
    page_bufs = [pltpu.VMEM((PAGE_SLOTS, pages) + pool.shape[2:], pool.dtype) for pool in pools]
    n_keys = pages * PAGE_SIZE
    mla_rows = MLA_HEADS * n_tok
    diff_rows = DIFF_ROWS * n_tok
    lam_specs = [pl.BlockSpec((1, DIFF_HEAD_DIM), const)] * 4 + [
        pl.BlockSpec((1, 2 * DIFF_HEAD_DIM), const)]
    grid_spec = pltpu.PrefetchScalarGridSpec(
        num_scalar_prefetch=1,
        grid=(batch, nq),
        in_specs=[pl.BlockSpec((bq, DIFF_WIDTH), qrow),
                  pl.BlockSpec((seq, DIFF_KV_WIDTH), kv),
                  pl.BlockSpec((seq, DIFF_KV_WIDTH), kv),
                  pl.BlockSpec(dkmeta.shape, const),
                  pl.BlockSpec(dvmeta.shape, const)]
                 + lam_specs
                 + [pl.BlockSpec((n_tok, MLA_HEADS * MLA_KV_RANK), seq_row),
                    pl.BlockSpec((n_tok, MLA_HEADS * MLA_ROPE), seq_row),
                    pl.BlockSpec((n_tok, DIFF_WIDTH), seq_row),
                    pl.BlockSpec((n_tok, MLA_KV_RANK), seq_row),
                    pl.BlockSpec((n_tok, MLA_ROPE), seq_row),
                    pl.BlockSpec((n_tok, DIFF_KV_WIDTH), seq_row),
                    pl.BlockSpec((n_tok, DIFF_KV_WIDTH), seq_row)]
                 + lam_specs
                 + page_specs,
        out_specs=[pl.BlockSpec((bq, DIFF_WIDTH), qrow),
                   pl.BlockSpec((n_tok, MLA_HEADS * MLA_KV_RANK), seq_row),
                   pl.BlockSpec((n_tok, DIFF_WIDTH), seq_row)],
        scratch_shapes=[pltpu.VMEM((n_chunks, LANES, 2 * bq), BF16),
                        pltpu.VMEM((seq // bk, DIFF_KV_WIDTH, bk), BF16),
                        pltpu.VMEM((LANES, DIFF_KV_WIDTH), BF16),
                        pltpu.VMEM((DIFF_KV_WIDTH, LANES), BF16),
                        pltpu.VMEM((n_chunks, 2 * bq), F32),
                        pltpu.VMEM((n_chunks, 2 * bq), F32),
                        pltpu.VMEM((n_chunks, 2 * DIFF_HEAD_DIM, 2 * bq), F32),
                        pltpu.VMEM((n_chunks, bk, 2 * bq), F32),
                        pltpu.VMEM((n_chunks, bk, 2 * bq), BF16),
                        pltpu.VMEM((n_chunks, 2 * bq), F32),
                        pltpu.VMEM((mla_rows, MLA_KV_RANK), F32),
                        pltpu.VMEM((mla_rows, MLA_ROPE), F32),
                        pltpu.VMEM((diff_rows, DIFF_KV_WIDTH), F32),
                        pltpu.VMEM((n_keys, MLA_KV_RANK), BF16),
                        pltpu.VMEM((MLA_ROPE, n_keys), BF16),
                        pltpu.VMEM((DIFF_KV_WIDTH, n_keys), BF16),
                        pltpu.VMEM((DIFF_KV_HEADS, n_keys, 2 * DIFF_HEAD_DIM), BF16),
                        pltpu.VMEM((PAGE_SIZE, MLA_KV_RANK), F32),
                        pltpu.VMEM((PAGE_SIZE, MLA_ROPE), F32),
                        pltpu.VMEM((PAGE_SIZE, DIFF_KV_WIDTH), F32),
                        pltpu.VMEM((PAGE_SIZE, DIFF_KV_WIDTH), F32),
                        pltpu.VMEM((mla_rows, 1), F32),
                        pltpu.VMEM((mla_rows, 1), F32),
                        pltpu.VMEM((mla_rows, MLA_KV_RANK), F32),
                        pltpu.VMEM((diff_rows, 1), F32),
                        pltpu.VMEM((diff_rows, 1), F32),
                        pltpu.VMEM((diff_rows, 2 * DIFF_HEAD_DIM), F32)]
                       + page_bufs + [pltpu.SemaphoreType.DMA((PAGE_SLOTS,))],
    )
    rows = n_seq * n_tok
    return pl.pallas_call(
        functools.partial(
            _attn_kernel, n_seq=n_seq, n_steps=n_pages // pages,
            diff_kw=dict(bq=bq, bk=bk, lam_init=lam_init),
            paged_kw=dict(n_tok=n_tok, pages=pages, lam_init=lam_init)),
        grid_spec=grid_spec,
        out_shape=[jax.ShapeDtypeStruct((batch * seq, DIFF_WIDTH), BF16),
                   jax.ShapeDtypeStruct((rows, MLA_HEADS * MLA_KV_RANK), F32),
                   jax.ShapeDtypeStruct((rows, DIFF_WIDTH), F32)],
        compiler_params=pltpu.CompilerParams(
            dimension_semantics=("arbitrary", "arbitrary"), vmem_limit_bytes=VMEM_LIMIT),
        name="diff_and_paged_attn",
    )(page_table, dq, dkb, dvb, dkmeta, dvmeta, *lams, g_sub,
      qabs, qpe, dq_s, c_new, kr_new, dk_new, dv_new, *lams, g_sub, *pools)


def _out_proj_kernel(x_ref, lat_ref, diff_ref, w_uv_ref, g_mla_ref, w_out_ref, h_ref, mla_ref):
    for h in range(MLA_HEADS):
        lat = lat_ref[:, h * MLA_KV_RANK:(h + 1) * MLA_KV_RANK].astype(BF16)
        mla_ref[:, h * MLA_V:(h + 1) * MLA_V] = _dot(lat, w_uv_ref[h])
    mla = _rms(mla_ref[...], g_mla_ref[...], NORM_EPS).astype(BF16)
    h_ref[...] = (x_ref[...] + _dot(mla, w_out_ref[0:MLA_WIDTH, :])
                  + _dot(diff_ref[...].astype(BF16), w_out_ref[MLA_WIDTH:MLA_WIDTH + DIFF_WIDTH, :]))


def _out_proj(x, lat, diff, w, tm):
    rows, d = x.shape
    row = lambda i: (i, 0)
    return pl.pallas_call(
        _out_proj_kernel,
        grid=(rows // tm,),
        in_specs=[pl.BlockSpec((tm, d), row),
                  pl.BlockSpec((tm, lat.shape[1]), row),
                  pl.BlockSpec((tm, diff.shape[1]), row),
                  _const_spec(w["w_uv"].shape),
                  _const_spec((1, MLA_WIDTH)),
                  _const_spec(w["w_out"].shape)],
        out_specs=pl.BlockSpec((tm, d), row),
        out_shape=jax.ShapeDtypeStruct((rows, d), F32),
        scratch_shapes=[pltpu.VMEM((tm, MLA_WIDTH), F32)],
        compiler_params=pltpu.CompilerParams(
            dimension_semantics=("arbitrary",), vmem_limit_bytes=VMEM_LIMIT),
        name="out_proj",
    )(x, lat, diff, w["w_uv"], w["g_mla_out"], w["w_out"])


def _ffn_kernel(h_ref, g_ffn_ref, w_gate_ref, w_up_ref, w_down_ref, g_final_ref, o_ref, xn_ref):
    j = pl.program_id(1)

    @pl.when(j == 0)
    def _():
        h = h_ref[...]
        xn_ref[...] = _rms(h, g_ffn_ref[...], NORM_EPS).astype(BF16)
        o_ref[...] = h

    xn = xn_ref[...]
    gate = _dot(xn, w_gate_ref[...])
    up = _dot(xn, w_up_ref[...])
    act = (gate * (1.0 / (1.0 + jnp.exp(-gate))) * up).astype(BF16)
    o_ref[...] += _dot(act, w_down_ref[...])

    @pl.when(j == pl.num_programs(1) - 1)
    def _():
        o_ref[...] = _rms(o_ref[...], g_final_ref[...], NORM_EPS)


def _ffn(h, w, tm, th):
    rows, d = h.shape
    hidden = w["w_gate"].shape[1]
    row = lambda i, j: (i, 0)
    return pl.pallas_call(
        _ffn_kernel,
        grid=(rows // tm, hidden // th),
        in_specs=[pl.BlockSpec((tm, d), row),
                  pl.BlockSpec((1, d), lambda i, j: (0, 0)),
                  pl.BlockSpec((d, th), lambda i, j: (0, j)),
                  pl.BlockSpec((d, th), lambda i, j: (0, j)),
                  pl.BlockSpec((th, d), lambda i, j: (j, 0)),
                  pl.BlockSpec((1, d), lambda i, j: (0, 0))],
        out_specs=pl.BlockSpec((tm, d), row),
        out_shape=jax.ShapeDtypeStruct((rows, d), F32),
        scratch_shapes=[pltpu.VMEM((tm, d), BF16)],
        compiler_params=pltpu.CompilerParams(
            dimension_semantics=("arbitrary", "arbitrary"), vmem_limit_bytes=VMEM_LIMIT),
        name="ffn",
    )(h, w["g_ffn"], w["w_gate"], w["w_up"], w["w_down"], w["g_final"])


def _rope_tables(pos, rot_dim):
    half = rot_dim // 2
    inv = ROPE_THETA ** (-jnp.arange(half, dtype=F32) * 2.0 / rot_dim)
    ang = pos.astype(F32)[:, None] * inv
    d = jnp.arange(LANES) % DIFF_HEAD_DIM
    a = ang[:, d % half]
    cos = jnp.where(d < rot_dim, jnp.cos(a), 1.0)
    sin = jnp.sin(a)
    sin_lo = jnp.where(d < half, -sin, 0.0)
    sin_hi = jnp.where((d >= half) & (d < rot_dim), sin, 0.0)
    return cos.astype(F32), sin_lo.astype(F32), sin_hi.astype(F32)


def _tables(pos):
    return _rope_tables(pos, MLA_ROPE) + _rope_tables(pos, DIFF_ROT)


def _row_block(rows, target):
    tm = min(rows, target)
    while rows % tm:
        tm //= 2
    return tm


PROJ_ROWS = 256
DENSE_ROWS = 512
FFN_HIDDEN_CHUNK = 512
MLA_BLOCK = 256
DIFF_Q_BLOCK = 128
DIFF_K_BLOCK = 256
PAGES_PER_TILE = 16


def kernel(x_prompt, x_sample, cache_mla_latent, cache_mla_krope, cache_diff_k, cache_diff_v,
           page_table, meta_tokens, g_attn, w_in, g_q_a, w_q_b, g_kv_a, w_uk, w_uv, g_mla_out,
           lambda_q1, lambda_k1, lambda_q2, lambda_k2, g_subln, w_out, g_ffn, w_gate, w_up,
           w_down, g_final):
    batch, seq, d = x_prompt.shape
    n_seq, n_tok, _ = x_sample.shape
    depth = w_in.shape[0]
    assert depth == 1, "single-layer trunk"
    n_phys = cache_mla_latent.shape[1]
    past_len = page_table.shape[1] * PAGE_SIZE
    lam_init = 0.8 - 0.6 * math.exp(-0.3 * 0)

    wi = w_in[0]
    o = [0, MLA_Q_RANK, MLA_Q_RANK + MLA_KV_RANK, MLA_Q_RANK + MLA_KV_RANK + MLA_ROPE]
    o.append(o[3] + DIFF_WIDTH)
    o.append(o[4] + DIFF_KV_WIDTH)
    o.append(o[5] + DIFF_KV_WIDTH)
    w_in_r = jnp.concatenate(
        [wi[:, o[0]:o[2]], wi[:, o[3]:o[6]], wi[:, o[2]:o[3]],
         jnp.zeros((d, LANES - MLA_ROPE), wi.dtype)], axis=1).astype(BF16)
    wq = w_q_b[0]
    w_qb = jnp.concatenate(
        [wq[:, :, :MLA_QK_NOPE].reshape(MLA_Q_RANK, -1),
         wq[:, :, MLA_QK_NOPE:].reshape(MLA_Q_RANK, -1)], axis=1).astype(BF16)
    w = {
        "g_attn": g_attn[0][None], "w_in": w_in_r, "g_q_a": g_q_a[0][None],
        "g_kv_a": g_kv_a[0][None], "w_qb": w_qb,
        "w_uk": jnp.transpose(w_uk[0], (1, 2, 0)).astype(BF16),
        "w_uv": jnp.transpose(w_uv[0], (1, 0, 2)).astype(BF16),
        "g_mla_out": g_mla_out[0][None], "w_out": w_out[0].astype(BF16),
        "g_ffn": g_ffn[0][None], "w_gate": w_gate[0].astype(BF16),
        "w_up": w_up[0].astype(BF16), "w_down": w_down[0].astype(BF16),
        "g_final": g_final[None],
    }
    lams = (lambda_q1, lambda_k1, lambda_q2, lambda_k2)
    g_sub = g_subln[0][None]

    xp = x_prompt.reshape(batch * seq, d)
    xs = x_sample.reshape(n_seq * n_tok, d)
    tm_p = _row_block(seq, PROJ_ROWS)
    tm_s = _row_block(n_seq * n_tok, PROJ_ROWS)
    tab_p = _tables(N_META + jnp.arange(seq))
    tab_s = _tables(jnp.tile(past_len + jnp.arange(n_tok), tm_s // n_tok))
    tab_m = _tables(jnp.arange(N_META))
    (qabs_p, qpe_p, dq_p, c_p, kr_p, dk_p, dv_p, cb_p, krb_p, dkb_p, dvb_p) = _proj(
        xp, tab_p, tm_p, BF16, w)
    (qabs_s, qpe_s, dq_s, c_s, kr_s, dk_s, dv_s, _, _, _, _) = _proj(xs, tab_s, tm_s, F32, w)
    (_, _, _, c_m, kr_m, dk_m, dv_m, cb_m, krb_m, dkb_m, dvb_m) = _proj(
        meta_tokens.astype(F32), tab_m, N_META, BF16, w)

    lat_p = _mla_attn(qabs_p, qpe_p, cb_p, krb_p, cb_m, krb_m, batch, seq,
                      _row_block(seq, MLA_BLOCK))

    pools = (cache_mla_latent,
             jnp.swapaxes(cache_mla_krope, 2, 3),
             jnp.transpose(cache_diff_k, (0, 1, 3, 4, 5, 2)).reshape(
                 depth, n_phys, DIFF_KV_WIDTH, PAGE_SIZE),
             cache_diff_v.reshape(depth, n_phys, PAGE_SIZE * DIFF_KV_HEADS, 2 * DIFF_HEAD_DIM))
    pages = _row_block(page_table.shape[1], PAGES_PER_TILE)
    diff_p, lat_s, diff_s = _attn(
        dq_p, dkb_p, dvb_p, dkb_m, dvb_m, lams, g_sub, batch, seq,
        _row_block(seq, DIFF_Q_BLOCK), _row_block(seq, DIFF_K_BLOCK),
        page_table, qabs_s, qpe_s, dq_s, c_s, kr_s, dk_s, dv_s, pools, n_tok, pages, lam_init)

    h_p = _out_proj(xp, lat_p, diff_p, w, _row_block(batch * seq, DENSE_ROWS))
    h_s = _out_proj(xs, lat_s, diff_s, w, _row_block(n_seq * n_tok, DENSE_ROWS))
    hidden = w_gate.shape[2]
    th = FFN_HIDDEN_CHUNK if hidden % FFN_HIDDEN_CHUNK == 0 else hidden
    y_p = _ffn(h_p, w, _row_block(batch * seq, DENSE_ROWS), th).reshape(batch, seq, d)
    y_s = _ffn(h_s, w, _row_block(n_seq * n_tok, DENSE_ROWS), th).reshape(n_seq, n_tok, d)

    def with_meta(meta, real):
        n = real.shape[-1]
        full = jnp.concatenate(
            [jnp.broadcast_to(meta[None], (batch, N_META, n)), real.reshape(batch, seq, n)], axis=1)
        return full[None]

    t = seq + N_META
    kv_shape = (DIFF_KV_HEADS, 2, DIFF_HEAD_DIM)
    v_shape = (DIFF_KV_HEADS, 2 * DIFF_HEAD_DIM)
    return (y_p, y_s,
            with_meta(c_m, c_p), with_meta(kr_m, kr_p),
            with_meta(dk_m, dk_p).reshape((1, batch, t) + kv_shape),
            with_meta(dv_m, dv_p).reshape((1, batch, t) + v_shape),
            c_s.reshape(1, n_seq, n_tok, -1), kr_s.reshape(1, n_seq, n_tok, -1),
            dk_s.reshape((1, n_seq, n_tok) + kv_shape),
            dv_s.reshape((1, n_seq, n_tok) + v_shape))
```
